```python
import jax, jax.numpy as jnp
from jax import lax
import numpy as np

D_MODEL = 2048
BATCH = 8
SEQ = 2048
DEPTH = 1
DEC_BATCH = 128
DEC_SEQ = 4
PAST_LEN = 16384
PAGE_SIZE = 128

MLA_HEADS = 8
MLA_NOPE = 128
MLA_ROPE = 64
MLA_V = 128
MLA_Q_RANK = 512
MLA_KV_RANK = 256
NSA_HEADS = 8
NSA_KV_GROUPS = 2
NSA_HPG = NSA_HEADS // NSA_KV_GROUPS
NSA_DH = 128
CMP_BLOCK = 32
CMP_STRIDE = 16
SEL_BLOCK = 64
SEL_TOPK = 16
SEL_LOCAL = 2
WINDOW = 512
ROPE_THETA = 500000.0
NSA_ROT = NSA_DH // 4
D_FF = 5632
RMS_EPS = 1e-6
QUERY_BLOCK = 128
MIX_WIDTH = MLA_HEADS * MLA_V + NSA_HEADS * NSA_DH
MLA_SCALE = (MLA_NOPE + MLA_ROPE) ** -0.5
NSA_SCALE = NSA_DH ** -0.5
NEG_INF = -1e30
BIG = 1e6
TINY = 1e-30
KV_COLS = NSA_KV_GROUPS * NSA_DH
IN_SIZES = (MLA_Q_RANK, MLA_KV_RANK, MLA_ROPE, NSA_HEADS * NSA_DH,
            KV_COLS, KV_COLS, KV_COLS, KV_COLS, KV_COLS, KV_COLS, 3 * NSA_HEADS)
IN_WIDTH = sum(IN_SIZES)

kernel_name = 'hymba_mla_nsa_macaron_step'


def rmsnorm(x, g):
    xf = x.astype(jnp.float32)
    y = xf * lax.rsqrt(jnp.mean(xf * xf, axis=-1, keepdims=True) + RMS_EPS)
    return (y * g.astype(jnp.float32)).astype(x.dtype)


def swiglu(x, wg, wu, wd):
    return (jax.nn.silu(x @ wg) * (x @ wu)) @ wd


def rope_tables(pos, rot_dim):
    inv = ROPE_THETA ** (-jnp.arange(0, rot_dim, 2, dtype=jnp.float32) / rot_dim)
    ang = pos.astype(jnp.float32)[:, None] * inv[None, :]
    return jnp.cos(ang), jnp.sin(ang)


def apply_rope(x, cos, sin):
    half = cos.shape[-1]
    shape = (1, cos.shape[0]) + (1,) * (x.ndim - 3) + (half,)
    c = cos.reshape(shape).astype(x.dtype)
    s = sin.reshape(shape).astype(x.dtype)
    x1, x2 = x[..., :half], x[..., half:2 * half]
    return jnp.concatenate([x1 * c - x2 * s, x2 * c + x1 * s, x[..., 2 * half:]], axis=-1)


def masked_softmax(s, mask):
    s = jnp.where(mask, s, NEG_INF)
    e = jnp.exp(s - jnp.max(s, axis=-1, keepdims=True)) * mask
    return e / jnp.maximum(jnp.sum(e, axis=-1, keepdims=True), TINY)


def compress_rows(k, pos_w, lin_w):
    nc = (k.shape[0] - CMP_BLOCK) // CMP_STRIDE + 1
    r = CMP_BLOCK // CMP_STRIDE
    segs = k[:(nc + r - 1) * CMP_STRIDE].reshape((nc + r - 1, CMP_STRIDE) + k.shape[1:])
    w = pos_w.reshape(r, CMP_STRIDE, pos_w.shape[-1])
    acc = jnp.einsum('nsgd,sd->ngd', segs[:nc], w[0])
    for m in range(1, r):
        acc = acc + jnp.einsum('nsgd,sd->ngd', segs[m:m + nc], w[m])
    return jnp.einsum('ngd,de->nge', acc, lin_w)


def attend_sequence(q_lat, q_rope, q_nsa, gates, ckv, krope, k_cmp, v_cmp, k_sel, v_sel, k_win, v_win,
                    cmp_pos_k, cmp_lin_k, cmp_pos_v, cmp_lin_v):
    T, Q, Lw = ckv.shape[0], q_lat.shape[0], k_win.shape[0]
    G, DH = NSA_KV_GROUPS, NSA_DH
    win_base = T - Lw
    kc = compress_rows(k_cmp, cmp_pos_k, cmp_lin_k)
    vc = compress_rows(v_cmp, cmp_pos_v, cmp_lin_v)
    nc = kc.shape[0]
    c_start = jnp.arange(nc) * CMP_STRIDE
    cmp_last = c_start + CMP_BLOCK - 1
    ns = -(-T // SEL_BLOCK)
    s_start = jnp.arange(ns) * SEL_BLOCK
    overlap = ((c_start[:, None] < s_start[None, :] + SEL_BLOCK)
               & (c_start[:, None] + CMP_BLOCK > s_start[None, :])).astype(jnp.float32)
    pad_sel = ns * SEL_BLOCK - T

    def to_blocks(a):
        return jnp.pad(a, ((0, pad_sel), (0, 0), (0, 0))).reshape(ns, SEL_BLOCK, G, DH).transpose(2, 0, 1, 3)

    ks_b, vs_b = to_blocks(k_sel), to_blocks(v_sel)
    n_top = min(SEL_TOPK, ns)
    kw_pad = jnp.pad(k_win, ((WINDOW, 0), (0, 0), (0, 0)))
    vw_pad = jnp.pad(v_win, ((WINDOW, 0), (0, 0), (0, 0)))
    qb = min(Q, QUERY_BLOCK)
    nq = Q // qb
    key_pos = jnp.arange(T)
    blk = jnp.arange(ns)
    in_blk = jnp.arange(SEL_BLOCK)
    g_idx = jnp.arange(G)[None, :, None]
    win_off = jnp.arange(WINDOW + qb)

    def query_block(args):
        ql, qr, qn, gt, t = args
        s = (jnp.einsum('qhr,kr->qhk', ql, ckv)
             + jnp.einsum('qhd,kd->qhk', qr, krope)).astype(jnp.float32) * MLA_SCALE
        p = jax.nn.softmax(jnp.where(key_pos[None, None, :] <= t[:, None, None], s, NEG_INF), axis=-1)
        o_lat = jnp.einsum('qhk,kr->qhr', p.astype(ckv.dtype), ckv)
        qg = qn.reshape(qb, G, NSA_HPG, DH)
        sc = jnp.einsum('qgjd,cgd->qgjc', qg, kc).astype(jnp.float32) * NSA_SCALE
        pc = masked_softmax(sc, (cmp_last[None, :] <= t[:, None])[:, None, None, :])
        o_c = jnp.einsum('qgjc,cgd->qgjd', pc.astype(vc.dtype), vc)
        imp = jnp.einsum('qgjc,cn->qgn', pc, overlap)
        cur = t // SEL_BLOCK
        valid = blk[None, :] <= cur[:, None]
        forced = valid & ((blk[None, :] == 0) | (blk[None, :] > cur[:, None] - SEL_LOCAL))
        imp = jnp.where(forced[:, None, :], BIG, jnp.where(valid[:, None, :], imp, -BIG))
        _, idx = lax.top_k(imp, n_top)
        ks = ks_b[g_idx, idx]
        vs = vs_b[g_idx, idx]
        sel_pos = idx[..., None] * SEL_BLOCK + in_blk
        ss = jnp.einsum('qgjd,qgnld->qgjnl', qg, ks).astype(jnp.float32) * NSA_SCALE
        ss = jnp.where((sel_pos <= t[:, None, None, None])[:, :, None], ss, NEG_INF)
        ps = jax.nn.softmax(ss.reshape(qb, G, NSA_HPG, n_top * SEL_BLOCK), axis=-1)
        o_s = jnp.einsum('qgjm,qgmd->qgjd', ps.astype(vs.dtype), vs.reshape(qb, G, n_top * SEL_BLOCK, DH))
        l0 = t[0] - win_base
        kw = lax.dynamic_slice_in_dim(kw_pad, l0, WINDOW + qb, axis=0)
        vw = lax.dynamic_slice_in_dim(vw_pad, l0, WINDOW + qb, axis=0)
        w_pos = win_base + l0 - WINDOW + win_off
        w_mask = ((w_pos[None, :] >= win_base) & (w_pos[None, :] >= t[:, None] - WINDOW)
                  & (w_pos[None, :] <= t[:, None]))
        sw = jnp.einsum('qgjd,kgd->qgjk', qg, kw).astype(jnp.float32) * NSA_SCALE
        pw = jax.nn.softmax(jnp.where(w_mask[:, None, None, :], sw, NEG_INF), axis=-1)
        o_w = jnp.einsum('qgjk,kgd->qgjd', pw.astype(vw.dtype), vw)
        gg = gt.reshape(qb, G, NSA_HPG, 3)
        o_n = gg[..., 0:1] * o_c + gg[..., 1:2] * o_s + gg[..., 2:3] * o_w
        return o_lat, o_n.reshape(qb, NSA_HEADS, DH)

    q_pos = T - Q + jnp.arange(Q)

    def chunks(a):
        return a.reshape((nq, qb) + a.shape[1:])

    o_lat, o_n = lax.map(query_block, (chunks(q_lat), chunks(q_rope), chunks(q_nsa), chunks(gates), chunks(q_pos)))
    return o_lat.reshape(Q, MLA_HEADS, MLA_KV_RANK), o_n.reshape(Q, NSA_HEADS, NSA_DH)


def project_mixer_inputs(h, pos, w):
    B, S = h.shape[:2]
    cos_m, sin_m = rope_tables(pos, MLA_ROPE)
    cos_n, sin_n = rope_tables(pos, NSA_ROT)
    z = h @ w['w_in']
    zq, zkv, zkr, zqn, zkc, zvc, zks, zvs, zkw, zvw, zg = jnp.split(
        z, np.cumsum(IN_SIZES)[:-1].tolist(), axis=-1)
    cq = rmsnorm(zq, w['mla_q_norm'])
    q = (cq @ w['w_mla_q_up']).reshape(B, S, MLA_HEADS, MLA_NOPE + MLA_ROPE)
    q_rope = apply_rope(q[..., MLA_NOPE:], cos_m, sin_m)
    w_uk = w['w_mla_k_up'].reshape(MLA_KV_RANK, MLA_HEADS, MLA_NOPE)
    q_lat = jnp.einsum('bshn,rhn->bshr', q[..., :MLA_NOPE], w_uk)
    ckv = rmsnorm(zkv, w['mla_kv_norm'])
    krope = apply_rope(zkr, cos_m, sin_m)
    def kv(a):
        return a.reshape(B, S, NSA_KV_GROUPS, NSA_DH)
    q_nsa = apply_rope(zqn.reshape(B, S, NSA_HEADS, NSA_DH), cos_n, sin_n)
    k_cmp = apply_rope(kv(zkc), cos_n, sin_n)
    k_sel = apply_rope(kv(zks), cos_n, sin_n)
    k_win = apply_rope(kv(zkw), cos_n, sin_n)
    gates = jax.nn.sigmoid(zg.reshape(B, S, NSA_HEADS, 3))
    return (q_lat, q_rope, q_nsa, gates), (ckv, krope, k_cmp, kv(zvc), k_sel, kv(zvs), k_win, kv(zvw))


def decoder_layer(x, pos, attend, w):
    x = x + 0.5 * swiglu(rmsnorm(x, w['ffn1_norm']), w['w_ffn1_gate'], w['w_ffn1_up'], w['w_ffn1_down'])
    h = rmsnorm(x, w['mix_norm'])
    queries, rows = project_mixer_inputs(h, pos, w)
    o_lat, o_nsa = attend(queries, rows)
    B, S = x.shape[:2]
    w_uv = w['w_mla_v_up'].reshape(MLA_KV_RANK, MLA_HEADS, MLA_V)
    o_mla = jnp.einsum('bshr,rhv->bshv', o_lat, w_uv)
    mixed = jnp.concatenate([o_mla.reshape(B, S, -1), o_nsa.reshape(B, S, -1)], axis=-1)
    x = x + mixed @ w['w_out']
    x = x + 0.5 * swiglu(rmsnorm(x, w['ffn2_norm']), w['w_ffn2_gate'], w['w_ffn2_up'], w['w_ffn2_down'])
    return x, rows


def setup_inputs(seed: int = 0) -> dict:
    key = jax.random.key(seed)
    keys = iter(jax.random.split(key, 48))
    f32 = jnp.float32
    n_pages = PAST_LEN // PAGE_SIZE
    n_pool = (5 * DEC_BATCH * n_pages) // 4
    win_buf = min(WINDOW, PAST_LEN)
    L = DEPTH
    G, DH = NSA_KV_GROUPS, NSA_DH

    def nrm(shape, scale=1.0):
        return jax.random.normal(next(keys), shape, f32) * scale

    def gain(shape):
        return 1.0 + 0.1 * nrm(shape)

    page_table = jax.random.permutation(next(keys), n_pool)[:DEC_BATCH * n_pages]
    page_table = page_table.reshape(DEC_BATCH, n_pages).astype(jnp.int32)
    return {
        'x_prompt': nrm((BATCH, SEQ, D_MODEL)),
        'x_sample': nrm((DEC_BATCH, DEC_SEQ, D_MODEL)),
        'cache_mla_ckv': nrm((L, n_pool, PAGE_SIZE, MLA_KV_RANK)),
        'cache_mla_krope': nrm((L, n_pool, PAGE_SIZE, MLA_ROPE)),
        'cache_nsa_k_cmp': nrm((L, n_pool, PAGE_SIZE, G, DH)),
        'cache_nsa_v_cmp': nrm((L, n_pool, PAGE_SIZE, G, DH)),
        'cache_nsa_k_sel': nrm((L, n_pool, PAGE_SIZE, G, DH)),
        'cache_nsa_v_sel': nrm((L, n_pool, PAGE_SIZE, G, DH)),
        'state_nsa_k_win': nrm((L, DEC_BATCH, win_buf, G, DH)),
        'state_nsa_v_win': nrm((L, DEC_BATCH, win_buf, G, DH)),
        'page_table': page_table,
        'ffn1_norm': gain((L, D_MODEL)),
        'w_ffn1_gate': nrm((L, D_MODEL, D_FF), D_MODEL ** -0.5),
        'w_ffn1_up': nrm((L, D_MODEL, D_FF), D_MODEL ** -0.5),
        'w_ffn1_down': nrm((L, D_FF, D_MODEL), D_FF ** -0.5),
        'mix_norm': gain((L, D_MODEL)),
        'w_in': nrm((L, D_MODEL, IN_WIDTH), D_MODEL ** -0.5),
        'mla_q_norm': gain((L, MLA_Q_RANK)),
        'w_mla_q_up': nrm((L, MLA_Q_RANK, MLA_HEADS * (MLA_NOPE + MLA_ROPE)), MLA_Q_RANK ** -0.5),
        'mla_kv_norm': gain((L, MLA_KV_RANK)),
        'w_mla_k_up': nrm((L, MLA_KV_RANK, MLA_HEADS * MLA_NOPE), MLA_KV_RANK ** -0.5),
        'w_mla_v_up': nrm((L, MLA_KV_RANK, MLA_HEADS * MLA_V), MLA_KV_RANK ** -0.5),
        'nsa_cmp_pos_k': gain((L, CMP_BLOCK, DH)) * CMP_BLOCK ** -0.5,
        'nsa_cmp_lin_k': nrm((L, DH, DH), DH ** -0.5),
        'nsa_cmp_pos_v': gain((L, CMP_BLOCK, DH)) * CMP_BLOCK ** -0.5,
        'nsa_cmp_lin_v': nrm((L, DH, DH), DH ** -0.5),
        'w_out': nrm((L, MIX_WIDTH, D_MODEL), MIX_WIDTH ** -0.5),
        'ffn2_norm': gain((L, D_MODEL)),
        'w_ffn2_gate': nrm((L, D_MODEL, D_FF), D_MODEL ** -0.5),
        'w_ffn2_up': nrm((L, D_MODEL, D_FF), D_MODEL ** -0.5),
        'w_ffn2_down': nrm((L, D_FF, D_MODEL), D_FF ** -0.5),
        'final_norm': gain((D_MODEL,)),
    }


def reference(x_prompt, x_sample, cache_mla_ckv, cache_mla_krope, cache_nsa_k_cmp, cache_nsa_v_cmp,
              cache_nsa_k_sel, cache_nsa_v_sel, state_nsa_k_win, state_nsa_v_win, page_table,
              ffn1_norm, w_ffn1_gate, w_ffn1_up, w_ffn1_down, mix_norm, w_in, mla_q_norm, w_mla_q_up,
              mla_kv_norm, w_mla_k_up, w_mla_v_up, nsa_cmp_pos_k, nsa_cmp_lin_k, nsa_cmp_pos_v,
              nsa_cmp_lin_v, w_out, ffn2_norm, w_ffn2_gate, w_ffn2_up, w_ffn2_down, final_norm):
    paged = (cache_mla_ckv, cache_mla_krope, cache_nsa_k_cmp, cache_nsa_v_cmp, cache_nsa_k_sel, cache_nsa_v_sel)
    pos_p = jnp.arange(x_prompt.shape[1])
    pos_s = PAST_LEN + jnp.arange(x_sample.shape[1])
    xp, xs = x_prompt, x_sample
    new_p, new_s = [], []
    for layer in range(DEPTH):
        lw = dict(ffn1_norm=ffn1_norm[layer], w_ffn1_gate=w_ffn1_gate[layer], w_ffn1_up=w_ffn1_up[layer],
                  w_ffn1_down=w_ffn1_down[layer], mix_norm=mix_norm[layer], w_in=w_in[layer],
                  mla_q_norm=mla_q_norm[layer], w_mla_q_up=w_mla_q_up[layer], mla_kv_norm=mla_kv_norm[layer],
                  w_mla_k_up=w_mla_k_up[layer], w_mla_v_up=w_mla_v_up[layer], w_out=w_out[layer],
                  ffn2_norm=ffn2_norm[layer], w_ffn2_gate=w_ffn2_gate[layer], w_ffn2_up=w_ffn2_up[layer],
                  w_ffn2_down=w_ffn2_down[layer])
        cmp = (nsa_cmp_pos_k[layer], nsa_cmp_lin_k[layer], nsa_cmp_pos_v[layer], nsa_cmp_lin_v[layer])

        def attend_prompt(queries, rows, cmp=cmp):
            def one(args):
                return attend_sequence(*args, *cmp)
            return lax.map(one, queries + rows)

        def attend_sample(queries, rows, layer=layer, cmp=cmp):
            def one(args):
                pt, q4, new, kw_buf, vw_buf = args
                full = tuple(jnp.concatenate([pool[layer, pt].reshape((-1,) + pool.shape[3:]), r], axis=0)
                             for pool, r in zip(paged, new[:6]))
                k_win = jnp.concatenate([kw_buf, new[6]], axis=0)
                v_win = jnp.concatenate([vw_buf, new[7]], axis=0)
                return attend_sequence(*q4, *full, k_win, v_win, *cmp)
            return lax.map(one, (page_table, queries, rows, state_nsa_k_win[layer], state_nsa_v_win[layer]))

        xp, rows_p = decoder_layer(xp, pos_p, attend_prompt, lw)
        xs, rows_s = decoder_layer(xs, pos_s, attend_sample, lw)
        win_p = min(WINDOW, xp.shape[1])
        win_s = state_nsa_k_win.shape[2]
        kwin_s = jnp.concatenate([state_nsa_k_win[layer], rows_s[6]], axis=1)[:, -win_s:]
        vwin_s = jnp.concatenate([state_nsa_v_win[layer], rows_s[7]], axis=1)[:, -win_s:]
        new_p.append(rows_p[:6] + (rows_p[6][:, -win_p:], rows_p[7][:, -win_p:]))
        new_s.append(rows_s[:6] + (kwin_s, vwin_s))
    y_prompt = rmsnorm(xp, final_norm)
    y_sample = rmsnorm(xs, final_norm)

    def st(lst, i):
        return jnp.stack([r[i] for r in lst])

    return (y_prompt, y_sample,
            st(new_p, 0), st(new_s, 0), st(new_p, 1), st(new_s, 1),
            st(new_p, 2), st(new_s, 2), st(new_p, 3), st(new_s, 3),
            st(new_p, 4), st(new_s, 4), st(new_p, 5), st(new_s, 5),
            st(new_p, 6), st(new_s, 6), st(new_p, 7), st(new_s, 7))
```

```python
import functools

import jax
import jax.numpy as jnp
import numpy as np
from jax import lax
from jax.experimental import pallas as pl
from jax.experimental.pallas import tpu as pltpu

MLA_HEADS = 8
MLA_NOPE = 128
MLA_ROPE = 64
MLA_V = 128
MLA_Q_RANK = 512
MLA_KV_RANK = 256
NSA_HEADS = 8
NSA_KV_GROUPS = 2
NSA_HPG = NSA_HEADS // NSA_KV_GROUPS
NSA_DH = 128
CMP_BLOCK = 32
CMP_STRIDE = 16
SEL_BLOCK = 64
SEL_TOPK = 16
SEL_LOCAL = 2
WINDOW = 512
ROPE_THETA = 500000.0
NSA_ROT = NSA_DH // 4
RMS_EPS = 1e-6
QUERY_BLOCK = 128
MLA_SCALE = (MLA_NOPE + MLA_ROPE) ** -0.5
NSA_SCALE = NSA_DH ** -0.5
NEG_INF = -1e30
BIG = 1e6
TINY = 1e-30
KV_COLS = NSA_KV_GROUPS * NSA_DH
GATE_COLS = 3 * NSA_HEADS

COL_Q = 0
COL_KV = COL_Q + MLA_Q_RANK
COL_QN = COL_KV + MLA_KV_RANK
COL_KC = COL_QN + NSA_HEADS * NSA_DH
COL_VC = COL_KC + KV_COLS
COL_KS = COL_VC + KV_COLS
COL_VS = COL_KS + KV_COLS
COL_KW = COL_VS + KV_COLS
COL_VW = COL_KW + KV_COLS
COL_KR = COL_VW + KV_COLS
COL_G = COL_KR + MLA_ROPE
LANE = 128
IN_PAD = -(-(COL_G + GATE_COLS) // (3 * LANE)) * (3 * LANE)

QUERY_PAD = 8
NEW_PAD = 16
PAGE_ROWS = 128
SEL_SHIFT = SEL_BLOCK.bit_length() - 1
PAGES_PER_STEP = 16
VMEM_LIMIT = 56 * 1024 * 1024

F32 = jnp.float32
BF16 = jnp.bfloat16


def _dot(a, b):
    return jnp.dot(a, b, preferred_element_type=F32)


def _dot_nt(a, b):
    return lax.dot_general(a, b, (((1,), (1,)), ((), ())), preferred_element_type=F32)


def _rms(x, g):
    return x * lax.rsqrt(jnp.mean(x * x, axis=-1, keepdims=True) + RMS_EPS) * g


def _split3_dot_nt(w_bf16, x):
    hi = x.astype(BF16)
    r1 = x - hi.astype(F32)
    mid = r1.astype(BF16)
    lo = (r1 - mid.astype(F32)).astype(BF16)
    return _dot_nt(w_bf16, hi) + _dot_nt(w_bf16, mid) + _dot_nt(w_bf16, lo)


def _split3_dot(x, w_bf16):
    hi = x.astype(BF16)
    r1 = x - hi.astype(F32)
    mid = r1.astype(BF16)
    lo = (r1 - mid.astype(F32)).astype(BF16)
    return _dot(hi, w_bf16) + _dot(mid, w_bf16) + _dot(lo, w_bf16)


def _row_tile(n, want):
    t = min(want, n)
    while n % t:
        t -= 8
    return t


def _params(*sem):
    return pltpu.CompilerParams(dimension_semantics=sem, vmem_limit_bytes=VMEM_LIMIT)


def _ffn_kernel(x_ref, g_ref, wg_ref, wu_ref, wd_ref, *rest, final):
    if final:
        fg_ref, o_ref, h_ref, acc_ref = rest
    else:
        o_ref, h_ref, acc_ref = rest
    j = pl.program_id(1)

    @pl.when(j == 0)
    def _():
        h_ref[...] = _rms(x_ref[...], g_ref[...]).astype(BF16)
        acc_ref[...] = jnp.zeros_like(acc_ref)

    h = h_ref[...]
    a = _dot(h, wg_ref[...])
    u = _dot(h, wu_ref[...])
    act = (a / (1.0 + jnp.exp(-a))) * u
    acc_ref[...] += _dot(act.astype(BF16), wd_ref[...])

    @pl.when(j == pl.num_programs(1) - 1)
    def _():
        y = x_ref[...] + 0.5 * acc_ref[...]
        if final:
            y = _rms(y, fg_ref[...])
        o_ref[...] = y


def _ffn(x, g, wg, wu, wd, final_g=None):
    n, d = x.shape
    dff = wg.shape[1]
    tm = _row_tile(n, 512)
    tf = 512 if dff % 512 == 0 else dff
    in_specs = [
        pl.BlockSpec((tm, d), lambda i, j: (i, 0)),
        pl.BlockSpec((1, d), lambda i, j: (0, 0)),
        pl.BlockSpec((d, tf), lambda i, j: (0, j)),
        pl.BlockSpec((d, tf), lambda i, j: (0, j)),
        pl.BlockSpec((tf, d), lambda i, j: (j, 0)),
    ]
    args = [x, g.reshape(1, d), wg, wu, wd]
    if final_g is not None:
        in_specs.append(pl.BlockSpec((1, d), lambda i, j: (0, 0)))
        args.append(final_g.reshape(1, d))
    return pl.pallas_call(
        functools.partial(_ffn_kernel, final=final_g is not None),
        grid=(n // tm, dff // tf),
        in_specs=in_specs,
        out_specs=pl.BlockSpec((tm, d), lambda i, j: (i, 0)),
        out_shape=jax.ShapeDtypeStruct((n, d), F32),
        scratch_shapes=[pltpu.VMEM((tm, d), BF16), pltpu.VMEM((tm, d), F32)],
        compiler_params=_params("parallel", "arbitrary"),
        name="ffn",
    )(*args)


def _inproj_kernel(x_ref, g_ref, w_ref, o_ref, h_ref):
    @pl.when(pl.program_id(1) == 0)
    def _():
        h_ref[...] = _rms(x_ref[...], g_ref[...]).astype(BF16)

    o_ref[...] = _dot(h_ref[...], w_ref[...])


def _inproj(x, g, w):
    n, d = x.shape
    nout = w.shape[1]
    tm = _row_tile(n, 512)
    tn = nout // 3
    return pl.pallas_call(
        _inproj_kernel,
        grid=(n // tm, nout // tn),
        in_specs=[
            pl.BlockSpec((tm, d), lambda i, j: (i, 0)),
            pl.BlockSpec((1, d), lambda i, j: (0, 0)),
            pl.BlockSpec((d, tn), lambda i, j: (0, j)),
        ],
        out_specs=pl.BlockSpec((tm, tn), lambda i, j: (i, j)),
        out_shape=jax.ShapeDtypeStruct((n, nout), F32),
        scratch_shapes=[pltpu.VMEM((tm, d), BF16)],
        compiler_params=_params("parallel", "arbitrary"),
        name="inproj",
    )(x, g.reshape(1, d), w)


def _mlaprep_kernel(zq_ref, zkv_ref, gq_ref, gkv_ref, wqn_ref, wqr_ref, wuk_ref, ql_ref, qr_ref, ckv_ref):
    cq = _rms(zq_ref[...], gq_ref[...]).astype(BF16)
    qn = _dot(cq, wqn_ref[...]).astype(BF16)
    for h in range(MLA_HEADS):
        ql_ref[h] = _dot(qn[:, h * MLA_NOPE:(h + 1) * MLA_NOPE], wuk_ref[h]).astype(BF16)
    qr_ref[...] = _dot(cq, wqr_ref[...])
    ckv_ref[...] = _rms(zkv_ref[...], gkv_ref[...])


def _mlaprep(z, gq, gkv, wqn, wqr, wuk):
    n = z.shape[0]
    tm = _row_tile(n, 512)
    return pl.pallas_call(
        _mlaprep_kernel,
        grid=(n // tm,),
        in_specs=[
            pl.BlockSpec((tm, MLA_Q_RANK), lambda i: (i, COL_Q // MLA_Q_RANK)),
            pl.BlockSpec((tm, MLA_KV_RANK), lambda i: (i, COL_KV // MLA_KV_RANK)),
            pl.BlockSpec((1, MLA_Q_RANK), lambda i: (0, 0)),
            pl.BlockSpec((1, MLA_KV_RANK), lambda i: (0, 0)),
            pl.BlockSpec(wqn.shape, lambda i: (0, 0)),
            pl.BlockSpec(wqr.shape, lambda i: (0, 0)),
            pl.BlockSpec(wuk.shape, lambda i: (0, 0, 0)),
        ],
        out_specs=[
            pl.BlockSpec((MLA_HEADS, tm, MLA_KV_RANK), lambda i: (0, i, 0)),
            pl.BlockSpec((tm, MLA_HEADS * MLA_ROPE), lambda i: (i, 0)),
            pl.BlockSpec((tm, MLA_KV_RANK), lambda i: (i, 0)),
        ],
        out_shape=[
            jax.ShapeDtypeStruct((MLA_HEADS, n, MLA_KV_RANK), BF16),
            jax.ShapeDtypeStruct((n, MLA_HEADS * MLA_ROPE), F32),
            jax.ShapeDtypeStruct((n, MLA_KV_RANK), F32),
        ],
        compiler_params=_params("parallel"),
        name="mlaprep",
    )(z, z, gq.reshape(1, -1), gkv.reshape(1, -1), wqn, wqr, wuk)


def _mla_prompt_kernel(ql_ref, qr_ref, ckv_ref, kr_ref, wuv_ref, o_ref, m_ref, l_ref, acc_ref, *, tq, kc):
    q0 = pl.program_id(1) * tq
    hh = MLA_HEADS
    ql = ql_ref[...].reshape(hh * tq, MLA_KV_RANK)
    qr = qr_ref[...].reshape(hh * tq, MLA_ROPE)
    m_ref[...] = jnp.full(m_ref.shape, NEG_INF, F32)
    l_ref[...] = jnp.zeros_like(l_ref)
    acc_ref[...] = jnp.zeros_like(acc_ref)
    t = q0 + lax.broadcasted_iota(jnp.int32, (1, tq, kc), 1)
    koff = lax.broadcasted_iota(jnp.int32, (1, tq, kc), 2)

    def body(c, carry):
        k0 = pl.multiple_of(c * kc, kc)
        kk = ckv_ref[pl.ds(k0, kc), :]
        kr = kr_ref[pl.ds(k0, kc), :]
        s = (_dot_nt(ql, kk) + _dot_nt(qr, kr)) * MLA_SCALE
        s = jnp.where(koff + k0 <= t, s.reshape(hh, tq, kc), NEG_INF)
        m_prev = m_ref[...]
        m_new = jnp.maximum(m_prev, jnp.max(s, axis=-1, keepdims=True))
        alpha = jnp.exp(m_prev - m_new)
        p = jnp.exp(s - m_new)
        l_ref[...] = alpha * l_ref[...] + jnp.sum(p, axis=-1, keepdims=True)
        pv = _dot(p.reshape(hh * tq, kc).astype(BF16), kk)
        acc_ref[...] = alpha * acc_ref[...] + pv.reshape(hh, tq, MLA_KV_RANK)
        m_ref[...] = m_new
        return carry

    lax.fori_loop(0, (q0 + tq + kc - 1) // kc, body, 0)
    o = (acc_ref[...] / l_ref[...]).astype(BF16)
    for h in range(hh):
        o_ref[:, h * MLA_V:(h + 1) * MLA_V] = _dot(o[h], wuv_ref[h]).astype(o_ref.dtype)


def _mla_prompt(ql, qr, ckv, kr, wuv, nb, seq):
    tq = min(QUERY_BLOCK, seq)
    kc = min(512, seq)
    nq = seq // tq
    hh = MLA_HEADS
    return pl.pallas_call(
        functools.partial(_mla_prompt_kernel, tq=tq, kc=kc),
        grid=(nb, nq),
        in_specs=[
            pl.BlockSpec((hh, tq, MLA_KV_RANK), lambda b, i: (0, b * nq + i, 0)),
            pl.BlockSpec((hh, tq, MLA_ROPE), lambda b, i: (0, b * nq + i, 0)),
            pl.BlockSpec((seq, MLA_KV_RANK), lambda b, i: (b, 0)),
            pl.BlockSpec((seq, MLA_ROPE), lambda b, i: (b, 0)),
            pl.BlockSpec(wuv.shape, lambda b, i: (0, 0, 0)),
        ],
        out_specs=pl.BlockSpec((tq, hh * MLA_V), lambda b, i: (b * nq + i, 0)),
        out_shape=jax.ShapeDtypeStruct((nb * seq, hh * MLA_V), BF16),
        scratch_shapes=[
            pltpu.VMEM((hh, tq, 1), F32),
            pltpu.VMEM((hh, tq, 1), F32),
            pltpu.VMEM((hh, tq, MLA_KV_RANK), F32),
        ],
        compiler_params=_params("parallel", "arbitrary"),
        name="mla_prompt",
    )(ql, qr, ckv, kr, wuv)


def _segment_sums(x_ref, g, pw_ref, rows):
    a0 = jnp.zeros((rows, NSA_DH), F32)
    a1 = jnp.zeros((rows, NSA_DH), F32)
    for s in range(CMP_STRIDE):
        xs = x_ref[pl.ds(NSA_KV_GROUPS * s + g, rows, stride=NSA_KV_GROUPS * CMP_STRIDE), :]
        a0 = a0 + xs * pw_ref[s:s + 1, :]
        a1 = a1 + xs * pw_ref[CMP_STRIDE + s:CMP_STRIDE + s + 1, :]
    return a0, a1


def _compress_prompt_kernel(k_ref, v_ref, pwk_ref, pwv_ref, link_ref, linv_ref, kc_ref, vc_ref, *, nseg):
    for x_ref, pw_ref, lin_ref, o_ref in ((k_ref, pwk_ref, link_ref, kc_ref), (v_ref, pwv_ref, linv_ref, vc_ref)):
        for g in range(NSA_KV_GROUPS):
            a0, a1 = _segment_sums(x_ref, g, pw_ref, nseg)
            acc = (a0 + pltpu.roll(a1, nseg - 1, 0)).astype(BF16)
            o_ref[g] = _dot(acc, lin_ref[...]).astype(BF16)


def _compress_prompt(k, v, pwk, pwv, link, linv, nb, seq):
    nseg = seq // CMP_STRIDE
    row = pl.BlockSpec((NSA_KV_GROUPS * seq, NSA_DH), lambda b: (b, 0))
    pw = pl.BlockSpec((CMP_BLOCK, NSA_DH), lambda b: (0, 0))
    lin = pl.BlockSpec((NSA_DH, NSA_DH), lambda b: (0, 0))
    out = pl.BlockSpec((None, NSA_KV_GROUPS, nseg, NSA_DH), lambda b: (b, 0, 0, 0))
    shp = jax.ShapeDtypeStruct((nb, NSA_KV_GROUPS, nseg, NSA_DH), BF16)
    return pl.pallas_call(
        functools.partial(_compress_prompt_kernel, nseg=nseg),
        grid=(nb,),
        in_specs=[row, row, pw, pw, lin, lin],
        out_specs=[out, out],
        out_shape=[shp, shp],
        compiler_params=_params("parallel"),
        name="compress_prompt",
    )(k.reshape(-1, NSA_DH), v.reshape(-1, NSA_DH), pwk, pwv, link, linv)


def _masked_softmax(s, mask):
    s = jnp.where(mask, s, NEG_INF)
    e = jnp.where(mask, jnp.exp(s - jnp.max(s, axis=-1, keepdims=True)), 0.0)
    return e / jnp.maximum(jnp.sum(e, axis=-1, keepdims=True), TINY)


def _forced_importance(imp, blk, cur, ns):
    valid = blk <= cur
    forced = valid & ((blk == 0) | (blk > cur - SEL_LOCAL))
    imp = jnp.where(forced, BIG, jnp.where(valid, imp, -BIG))
    return jnp.where(blk < ns, imp, -2.0 * BIG)


def _nsa_prompt_kernel(qn_ref, kcmp_ref, vcmp_ref, ks_ref, vs_ref, kw_ref, vw_ref, zg_ref, o_ref,
                       m_ref, l_ref, acc_ref, *, tq, seq, kc, ntop):
    g = pl.program_id(1)
    q0 = pl.program_id(2) * tq
    jj = NSA_HPG
    nseg = seq // CMP_STRIDE
    ns = seq // SEL_BLOCK
    q = qn_ref[...].reshape(jj * tq, NSA_DH)

    t3 = q0 + lax.broadcasted_iota(jnp.int32, (1, tq, nseg), 1)
    cmp_last = lax.broadcasted_iota(jnp.int32, (1, tq, nseg), 2) * CMP_STRIDE + (CMP_BLOCK - 1)
    sc = (_dot_nt(q, kcmp_ref[...]) * NSA_SCALE).reshape(jj, tq, nseg)
    pc = _masked_softmax(sc, cmp_last <= t3)
    o_c = _dot(pc.reshape(jj * tq, nseg).astype(BF16), vcmp_ref[...])

    psum = jnp.sum(pc, axis=0)
    blk_r = lax.broadcasted_iota(jnp.int32, (ns, nseg), 0) * SEL_BLOCK
    c_st = lax.broadcasted_iota(jnp.int32, (ns, nseg), 1) * CMP_STRIDE
    ov_t = jnp.where((c_st < blk_r + SEL_BLOCK) & (c_st + CMP_BLOCK > blk_r), 1.0, 0.0).astype(BF16)
    imp_t = _split3_dot_nt(ov_t, psum)
    blk = lax.broadcasted_iota(jnp.int32, (ns, tq), 0)
    cur = (q0 + lax.broadcasted_iota(jnp.int32, (ns, tq), 1)) >> SEL_SHIFT
    imp_t = _forced_importance(imp_t, blk, cur, ns)
    rank = jnp.zeros((ns, tq), F32)
    for mm in range(ns):
        row = imp_t[mm:mm + 1, :]
        tie = jnp.where(blk > mm, 1.0, 0.0)
        rank = rank + jnp.where(row > imp_t, 1.0, jnp.where(row == imp_t, tie, 0.0))
    sel_t = jnp.where(rank < ntop, 1.0, 0.0)
    sel = jnp.concatenate([sel_t, jnp.zeros((LANE - ns, tq), F32)], axis=0).T.astype(BF16)

    m_ref[...] = jnp.full(m_ref.shape, NEG_INF, F32)
    l_ref[...] = jnp.zeros_like(l_ref)
    acc_ref[...] = jnp.zeros_like(acc_ref)
    tk = q0 + lax.broadcasted_iota(jnp.int32, (1, tq, kc), 1)
    koff = lax.broadcasted_iota(jnp.int32, (1, tq, kc), 2)
    e_row = lax.broadcasted_iota(jnp.int32, (LANE, kc), 0)
    e_col = lax.broadcasted_iota(jnp.int32, (LANE, kc), 1) >> SEL_SHIFT

    def body(c, carry):
        k0 = pl.multiple_of(c * kc, kc)
        kk = ks_ref[pl.ds(k0, kc), :]
        vv = vs_ref[pl.ds(k0, kc), :]
        expand = jnp.where(e_row == e_col + c * (kc // SEL_BLOCK), 1.0, 0.0).astype(BF16)
        chosen = _dot(sel, expand).reshape(1, tq, kc)
        s = (_dot_nt(q, kk) * NSA_SCALE).reshape(jj, tq, kc)
        s = jnp.where((chosen > 0.5) & (koff + k0 <= tk), s, NEG_INF)
        m_prev = m_ref[...]
        m_new = jnp.maximum(m_prev, jnp.max(s, axis=-1, keepdims=True))
        alpha = jnp.exp(m_prev - m_new)
        p = jnp.exp(s - m_new)
        l_ref[...] = alpha * l_ref[...] + jnp.sum(p, axis=-1, keepdims=True)
        pv = _dot(p.reshape(jj * tq, kc).astype(BF16), vv)
        acc_ref[...] = alpha * acc_ref[...] + pv.reshape(jj, tq, NSA_DH)
        m_ref[...] = m_new
        return carry

    lax.fori_loop(0, (q0 + tq + kc - 1) // kc, body, 0)
    o_s = acc_ref[...] / l_ref[...]

    wl = min(WINDOW + tq, seq)
    w0 = pl.multiple_of(jnp.clip(q0 - WINDOW, 0, seq - wl), tq)
    kw = kw_ref[pl.ds(w0, wl), :]
    vw = vw_ref[pl.ds(w0, wl), :]
    tw = q0 + lax.broadcasted_iota(jnp.int32, (1, tq, wl), 1)
    wpos = w0 + lax.broadcasted_iota(jnp.int32, (1, tq, wl), 2)
    sw = (_dot_nt(q, kw) * NSA_SCALE).reshape(jj, tq, wl)
    sw = jnp.where((wpos >= tw - WINDOW) & (wpos <= tw), sw, NEG_INF)
    pw = jnp.exp(sw - jnp.max(sw, axis=-1, keepdims=True))
    pw = pw / jnp.sum(pw, axis=-1, keepdims=True)
    o_w = _dot(pw.reshape(jj * tq, wl).astype(BF16), vw).reshape(jj, tq, NSA_DH)

    o_c = o_c.reshape(jj, tq, NSA_DH)
    zg = zg_ref[...]
    lane = lax.broadcasted_iota(jnp.int32, zg.shape, 1)
    for j in range(jj):
        def gate(k, j=j):
            col = jnp.sum(jnp.where(lane == COL_G % LANE + (g * jj + j) * 3 + k, zg, 0.0), axis=-1, keepdims=True)
            return 1.0 / (1.0 + jnp.exp(-col))
        o = gate(0) * o_c[j] + gate(1) * o_s[j] + gate(2) * o_w[j]
        o_ref[:, j * NSA_DH:(j + 1) * NSA_DH] = o.astype(o_ref.dtype)


def _nsa_prompt(qn, kcmp, vcmp, ks, vs, kw, vw, z, nb, seq):
    tq = min(QUERY_BLOCK, seq)
    kc = min(512, seq)
    nq = seq // tq
    jj = NSA_HPG
    nseg = seq // CMP_STRIDE
    ntop = min(SEL_TOPK, seq // SEL_BLOCK)
    rows = pl.BlockSpec((seq, NSA_DH), lambda b, g, i: (b, g))
    cmp = pl.BlockSpec((None, None, nseg, NSA_DH), lambda b, g, i: (b, g, 0, 0))
    return pl.pallas_call(
        functools.partial(_nsa_prompt_kernel, tq=tq, seq=seq, kc=kc, ntop=ntop),
        grid=(nb, NSA_KV_GROUPS, nq),
        in_specs=[
            pl.BlockSpec((jj, tq, NSA_DH), lambda b, g, i: (g, b * nq + i, 0)),
            cmp, cmp, rows, rows, rows, rows,
            pl.BlockSpec((tq, LANE), lambda b, g, i: (b * nq + i, COL_G // LANE)),
        ],
        out_specs=pl.BlockSpec((tq, jj * NSA_DH), lambda b, g, i: (b * nq + i, g)),
        out_shape=jax.ShapeDtypeStruct((nb * seq, NSA_HEADS * NSA_DH), BF16),
        scratch_shapes=[
            pltpu.VMEM((jj, tq, 1), F32),
            pltpu.VMEM((jj, tq, 1), F32),
            pltpu.VMEM((jj, tq, NSA_DH), F32),
        ],
        compiler_params=_params("parallel", "parallel", "arbitrary"),
        name="nsa_prompt",
    )(qn, kcmp, vcmp, ks, vs, kw, vw, z)


def _sample_select_kernel(pt_ref, *refs, pps, nch, past, nq, ns, ntop):
    del pt_ref
    kp = refs[:pps]
    vp = refs[pps:2 * pps]
    (qn_ref, pwk_ref, pwv_ref, link_ref, linv_ref, ov_ref, oc_ref, sel_ref,
     a0k_ref, a1k_ref, a0v_ref, a1v_ref) = refs[2 * pps:]
    c = pl.program_id(1)
    segs = PAGE_ROWS // CMP_STRIDE
    nseg = nch * pps * segs
    for p in range(pps):
        r0 = pl.multiple_of(c * (pps * segs) + p * segs, segs)
        for pg, pw_ref, a0_ref, a1_ref in ((kp[p], pwk_ref, a0k_ref, a1k_ref), (vp[p], pwv_ref, a0v_ref, a1v_ref)):
            for g in range(NSA_KV_GROUPS):
                a0, a1 = _segment_sums(pg, g, pw_ref, segs)
                a0_ref[g, pl.ds(r0, segs), :] = a0
                a1_ref[g, pl.ds(r0, segs), :] = a1

    @pl.when(c == nch - 1)
    def _():
        qp = QUERY_PAD
        rows = NSA_HPG * qp
        qi = lax.broadcasted_iota(jnp.int32, (1, qp, nseg), 1)
        t3 = past + jnp.minimum(qi, nq - 1)
        cmp_last = lax.broadcasted_iota(jnp.int32, (1, qp, nseg), 2) * CMP_STRIDE + (CMP_BLOCK - 1)
        imps = []
        for g in range(NSA_KV_GROUPS):
            acc_k = (a0k_ref[g] + pltpu.roll(a1k_ref[g], nseg - 1, 0)).astype(BF16)
            acc_v = (a0v_ref[g] + pltpu.roll(a1v_ref[g], nseg - 1, 0)).astype(BF16)
            kc = _dot(acc_k, link_ref[...]).astype(BF16)
            vc = _dot(acc_v, linv_ref[...]).astype(BF16)
            sc = (_dot_nt(qn_ref[g], kc) * NSA_SCALE).reshape(NSA_HPG, qp, nseg)
            pc = _masked_softmax(sc, cmp_last <= t3)
            oc_ref[g] = _dot(pc.reshape(rows, nseg).astype(BF16), vc)
            imps.append(_split3_dot(jnp.sum(pc, axis=0), ov_ref[...]))
        imp = jnp.concatenate(imps, axis=0)
        selw = imp.shape[1]
        blk = lax.broadcasted_iota(jnp.int32, imp.shape, 1)
        qrow = lax.broadcasted_iota(jnp.int32, imp.shape, 0) & (qp - 1)
        cur = (past + jnp.minimum(qrow, nq - 1)) >> SEL_SHIFT
        work = _forced_importance(imp, blk, cur, ns)
        blk_f = blk.astype(F32)
        chosen = jnp.zeros(imp.shape, F32)
        for _ in range(ntop):
            top = jnp.max(work, axis=-1, keepdims=True)
            first = jnp.min(jnp.where(work == top, blk_f, float(selw)), axis=-1, keepdims=True)
            hit = blk_f == first
            chosen = jnp.where(hit, 1.0, chosen)
            work = jnp.where(hit, -4.0 * BIG, work)
        sel_ref[...] = chosen


def _sample_select(page_table, kpool, vpool, qn, pwk, pwv, link, linv, ov, past, nq, ns, ntop):
    db, npages = page_table.shape
    pps = min(PAGES_PER_STEP, npages)
    nch = npages // pps
    nseg = npages * (PAGE_ROWS // CMP_STRIDE)
    rows = NSA_HPG * QUERY_PAD
    selw = ov.shape[1]
    pages = [pl.BlockSpec((None, NSA_KV_GROUPS * PAGE_ROWS, NSA_DH),
                          lambda b, c, pt, p=p: (pt[b * npages + c * pps + p], 0, 0)) for p in range(pps)]
    const2 = lambda b, c, pt: (0, 0)
    grid_spec = pltpu.PrefetchScalarGridSpec(
        num_scalar_prefetch=1,
        grid=(db, nch),
        in_specs=pages + pages + [
            pl.BlockSpec((None, NSA_KV_GROUPS, rows, NSA_DH), lambda b, c, pt: (b, 0, 0, 0)),
            pl.BlockSpec((CMP_BLOCK, NSA_DH), const2),
            pl.BlockSpec((CMP_BLOCK, NSA_DH), const2),
            pl.BlockSpec((NSA_DH, NSA_DH), const2),
            pl.BlockSpec((NSA_DH, NSA_DH), const2),
            pl.BlockSpec(ov.shape, const2),
        ],
        out_specs=[
            pl.BlockSpec((None, NSA_KV_GROUPS, rows, NSA_DH), lambda b, c, pt: (b, 0, 0, 0)),
            pl.BlockSpec((None, NSA_KV_GROUPS * QUERY_PAD, selw), lambda b, c, pt: (b, 0, 0)),
        ],
        scratch_shapes=[pltpu.VMEM((NSA_KV_GROUPS, nseg, NSA_DH), F32) for _ in range(4)],
    )
    return pl.pallas_call(
        functools.partial(_sample_select_kernel, pps=pps, nch=nch, past=past, nq=nq, ns=ns, ntop=ntop),
        grid_spec=grid_spec,
        out_shape=[
            jax.ShapeDtypeStruct((db, NSA_KV_GROUPS, rows, NSA_DH), F32),
            jax.ShapeDtypeStruct((db, NSA_KV_GROUPS * QUERY_PAD, selw), F32),
        ],
        compiler_params=_params("parallel", "arbitrary"),
        name="sample_select",
    )(page_table.reshape(-1), *([kpool] * pps), *([vpool] * pps), qn, pwk, pwv, link, linv, ov)


def _sample_attend_kernel(pt_ref, *refs, pps, nch, past, nq, win_buf):
    del pt_ref
    ckv_p = refs[:pps]
    kr_p = refs[pps:2 * pps]
    ks_p = refs[2 * pps:3 * pps]
    vs_p = refs[3 * pps:4 * pps]
    (ql_ref, qr_ref, qn_ref, selc_ref, sell_ref, oc_ref, gt_ref, ckvn_ref, krn_ref, ksn_ref, vsn_ref,
     kw_ref, vw_ref, e_ref, wuv_ref, omla_ref, onsa_ref,
     kc_s, kr_s, ks_s, vs_s, m1_ref, l1_ref, acc1_ref, m2_ref, l2_ref, acc2_ref) = refs[4 * pps:]
    c = pl.program_id(1)
    qp = QUERY_PAD
    jj = NSA_HPG
    hh = MLA_HEADS
    kk = pps * PAGE_ROWS

    @pl.when(c == 0)
    def _():
        m1_ref[...] = jnp.full(m1_ref.shape, NEG_INF, F32)
        l1_ref[...] = jnp.zeros_like(l1_ref)
        acc1_ref[...] = jnp.zeros_like(acc1_ref)
        m2_ref[...] = jnp.full(m2_ref.shape, NEG_INF, F32)
        l2_ref[...] = jnp.zeros_like(l2_ref)
        acc2_ref[...] = jnp.zeros_like(acc2_ref)

    for p in range(pps):
        sl = slice(p * PAGE_ROWS, (p + 1) * PAGE_ROWS)
        kc_s[sl, :] = ckv_p[p][...].astype(BF16)
        kr_s[sl, :] = kr_p[p][...].astype(BF16)
        ks_s[sl, :] = ks_p[p][...].astype(BF16)
        vs_s[sl, :] = vs_p[p][...].astype(BF16)

    def online(m_ref, l_ref, acc_ref, s, v):
        m_prev = m_ref[...]
        m_new = jnp.maximum(m_prev, jnp.max(s, axis=-1, keepdims=True))
        alpha = jnp.exp(m_prev - m_new)
        p = jnp.exp(s - m_new)
        l_ref[...] = alpha * l_ref[...] + jnp.sum(p, axis=-1, keepdims=True)
        acc_ref[...] = alpha * acc_ref[...] + _dot(p.astype(BF16), v)
        m_ref[...] = m_new

    ql = ql_ref[...]
    qr = qr_ref[...]
    kc = kc_s[...]
    s1 = (_dot_nt(ql, kc) + _dot_nt(qr, kr_s[...])) * MLA_SCALE
    online(m1_ref, l1_ref, acc1_ref, s1, kc)

    chosen = _dot(selc_ref[...].astype(BF16), e_ref[...])
    for g in range(NSA_KV_GROUPS):
        cols = slice(g * NSA_DH, (g + 1) * NSA_DH)
        qg = qn_ref[g]
        s2 = (_dot_nt(qg, ks_s[:, cols]) * NSA_SCALE).reshape(jj, qp, kk)
        s2 = jnp.where(chosen[g * qp:(g + 1) * qp].reshape(1, qp, kk) > 0.5, s2, NEG_INF).reshape(jj * qp, kk)
        online(m2_ref.at[g], l2_ref.at[g], acc2_ref.at[g], s2, vs_s[:, cols])

    @pl.when(c == nch - 1)
    def _():
        npad = NEW_PAD
        kidx = lax.broadcasted_iota(jnp.int32, (1, qp, npad), 2)
        qidx = lax.broadcasted_iota(jnp.int32, (1, qp, npad), 1)
        new_ok = (kidx <= qidx) & (kidx < nq)
        ckvn = ckvn_ref[...]
        s1n = (_dot_nt(ql, ckvn) + _dot_nt(qr, krn_ref[...])) * MLA_SCALE
        s1n = jnp.where(new_ok, s1n.reshape(hh, qp, npad), NEG_INF).reshape(hh * qp, npad)
        online(m1_ref, l1_ref, acc1_ref, s1n, ckvn)
        o_lat = (acc1_ref[...] / l1_ref[...]).astype(BF16)
        for h in range(hh):
            omla_ref[:, h * MLA_V:(h + 1) * MLA_V] = _dot(o_lat[h * qp:(h + 1) * qp], wuv_ref[h]).astype(omla_ref.dtype)

        wl = kw_ref.shape[0]
        widx = lax.broadcasted_iota(jnp.int32, (1, qp, wl), 2)
        wpos = past - win_buf + widx
        tw = past + jnp.minimum(lax.broadcasted_iota(jnp.int32, (1, qp, wl), 1), nq - 1)
        win_ok = (widx < win_buf + nq) & (wpos >= tw - WINDOW) & (wpos <= tw)
        sell = sell_ref[...]
        for g in range(NSA_KV_GROUPS):
            cols = slice(g * NSA_DH, (g + 1) * NSA_DH)
            qg = qn_ref[g]
            blk_ok = (sell[g * qp:(g + 1) * qp, 0:npad] > 0.5).reshape(1, qp, npad)
            s2n = (_dot_nt(qg, ksn_ref[:, cols]) * NSA_SCALE).reshape(jj, qp, npad)
            s2n = jnp.where(new_ok & blk_ok, s2n, NEG_INF).reshape(jj * qp, npad)
            online(m2_ref.at[g], l2_ref.at[g], acc2_ref.at[g], s2n, vsn_ref[:, cols])
            o_s = acc2_ref[g] / l2_ref[g]

            sw = (_dot_nt(qg, kw_ref[:, cols]) * NSA_SCALE).reshape(jj, qp, wl)
            sw = jnp.where(win_ok, sw, NEG_INF).reshape(jj * qp, wl)
            pw = jnp.exp(sw - jnp.max(sw, axis=-1, keepdims=True))
            pw = pw / jnp.sum(pw, axis=-1, keepdims=True)
            o_w = _dot(pw.astype(BF16), vw_ref[:, cols])

            gt = gt_ref[g]
            o = gt[:, 0:1] * oc_ref[g] + gt[:, 1:2] * o_s + gt[:, 2:3] * o_w
            for j in range(jj):
                hcol = (g * jj + j) * NSA_DH
                onsa_ref[:, hcol:hcol + NSA_DH] = o[j * qp:(j + 1) * qp].astype(onsa_ref.dtype)


def _sample_attend(page_table, ckv_pool, kr_pool, ks_pool, vs_pool, ql, qr, qn, selc, sell, oc, gt,
                   ckvn, krn, ksn, vsn, kw, vw, expand, wuv, past, nq, win_buf):
    db, npages = page_table.shape
    pps = min(PAGES_PER_STEP, npages)
    nch = npages // pps
    kk = pps * PAGE_ROWS
    qp = QUERY_PAD
    hh = MLA_HEADS
    rows = NSA_HPG * qp

    def pages(width):
        return [pl.BlockSpec((None, PAGE_ROWS, width),
                             lambda b, c, pt, p=p: (pt[b * npages + c * pps + p], 0, 0)) for p in range(pps)]

    def per_seq(shape):
        nd = len(shape)
        return pl.BlockSpec((None,) + tuple(shape), lambda b, c, pt: (b,) + (0,) * nd)

    def const(shape):
        nd = len(shape)
        return pl.BlockSpec(tuple(shape), lambda b, c, pt: (0,) * nd)

    in_specs = (
        pages(MLA_KV_RANK) + pages(MLA_ROPE) + pages(KV_COLS) + pages(KV_COLS) + [
            per_seq((hh * qp, MLA_KV_RANK)),
            per_seq((hh * qp, MLA_ROPE)),
            per_seq((NSA_KV_GROUPS, rows, NSA_DH)),
            pl.BlockSpec((None, None, NSA_KV_GROUPS * qp, LANE), lambda b, c, pt: (b, c, 0, 0)),
            per_seq((NSA_KV_GROUPS * qp, LANE)),
            per_seq((NSA_KV_GROUPS, rows, NSA_DH)),
            per_seq((NSA_KV_GROUPS, rows, 3)),
            per_seq((NEW_PAD, MLA_KV_RANK)),
            per_seq((NEW_PAD, MLA_ROPE)),
            per_seq((NEW_PAD, KV_COLS)),
            per_seq((NEW_PAD, KV_COLS)),
            per_seq(kw.shape[1:]),
            per_seq(vw.shape[1:]),
            const(expand.shape),
            const(wuv.shape),
        ])
    grid_spec = pltpu.PrefetchScalarGridSpec(
        num_scalar_prefetch=1,
        grid=(db, nch),
        in_specs=in_specs,
        out_specs=[per_seq((qp, hh * MLA_V)), per_seq((qp, NSA_HEADS * NSA_DH))],
        scratch_shapes=[
            pltpu.VMEM((kk, MLA_KV_RANK), BF16),
            pltpu.VMEM((kk, MLA_ROPE), BF16),
            pltpu.VMEM((kk, KV_COLS), BF16),
            pltpu.VMEM((kk, KV_COLS), BF16),
            pltpu.VMEM((hh * qp, 1), F32),
            pltpu.VMEM((hh * qp, 1), F32),
            pltpu.VMEM((hh * qp, MLA_KV_RANK), F32),
            pltpu.VMEM((NSA_KV_GROUPS, rows, 1), F32),
            pltpu.VMEM((NSA_KV_GROUPS, rows, 1), F32),
            pltpu.VMEM((NSA_KV_GROUPS, rows, NSA_DH), F32),
        ],
    )
    return pl.pallas_call(
        functools.partial(_sample_attend_kernel, pps=pps, nch=nch, past=past, nq=nq, win_buf=win_buf),
        grid_spec=grid_spec,
        out_shape=[
            jax.ShapeDtypeStruct((db, qp, hh * MLA_V), BF16),
            jax.ShapeDtypeStruct((db, qp, NSA_HEADS * NSA_DH), BF16),
        ],
        compiler_params=_params("parallel", "arbitrary"),
        name="sample_attend",
    )(page_table.reshape(-1), *([ckv_pool] * pps), *([kr_pool] * pps), *([ks_pool] * pps), *([vs_pool] * pps),
      ql, qr, qn, selc, sell, oc, gt, ckvn, krn, ksn, vsn, kw, vw, expand, wuv)


def _outproj_kernel(x_ref, a_ref, b_ref, wa_ref, wb_ref, o_ref):
    o_ref[...] = x_ref[...] + _dot(a_ref[...], wa_ref[...]) + _dot(b_ref[...], wb_ref[...])


def _outproj(x, a, b, wa, wb):
    n, d = x.shape
    tm = _row_tile(n, 512)
    return pl.pallas_call(
        _outproj_kernel,
        grid=(n // tm,),
        in_specs=[
            pl.BlockSpec((tm, d), lambda i: (i, 0)),
            pl.BlockSpec((tm, a.shape[1]), lambda i: (i, 0)),
            pl.BlockSpec((tm, b.shape[1]), lambda i: (i, 0)),
            pl.BlockSpec(wa.shape, lambda i: (0, 0)),
            pl.BlockSpec(wb.shape, lambda i: (0, 0)),
        ],
        out_specs=pl.BlockSpec((tm, d), lambda i: (i, 0)),
        out_shape=jax.ShapeDtypeStruct((n, d), F32),
        compiler_params=_params("parallel"),
        name="outproj",
    )(x, a, b, wa, wb)


def _rope_tables(pos, rot_dim):
    inv = ROPE_THETA ** (-jnp.arange(0, rot_dim, 2, dtype=F32) / rot_dim)
    ang = pos.astype(F32)[:, None] * inv[None, :]
    return jnp.cos(ang), jnp.sin(ang)


def _rope(x, cos, sin):
    half = cos.shape[-1]
    c = cos[:, None, :]
    s = sin[:, None, :]
    x1, x2 = x[..., :half], x[..., half:2 * half]
    return jnp.concatenate([x1 * c - x2 * s, x2 * c + x1 * s, x[..., 2 * half:]], axis=-1)


def _pad_queries(a, db, nq):
    heads, d = a.shape[1:]
    a = a.reshape(db, nq, heads, d).transpose(0, 2, 1, 3)
    a = jnp.pad(a, ((0, 0), (0, 0), (0, QUERY_PAD - nq), (0, 0)))
    return a.reshape(db, heads * QUERY_PAD, d)


def _pad_new(a, db, nq):
    a = a.reshape(db, nq, a.shape[-1])
    return jnp.pad(a, ((0, 0), (0, NEW_PAD - nq), (0, 0))).astype(BF16)


def kernel(x_prompt, x_sample, cache_mla_ckv, cache_mla_krope, cache_nsa_k_cmp, cache_nsa_v_cmp, cache_nsa_k_sel, cache_nsa_v_sel, state_nsa_k_win, state_nsa_v_win, page_table, ffn1_norm, w_ffn1_gate, w_ffn1_up, w_ffn1_down, mix_norm, w_in, mla_q_norm, w_mla_q_up, mla_kv_norm, w_mla_k_up, w_mla_v_up, nsa_cmp_pos_k, nsa_cmp_lin_k, nsa_cmp_pos_v, nsa_cmp_lin_v, w_out, ffn2_norm, w_ffn2_gate, w_ffn2_up, w_ffn2_down, final_norm):
    nb, seq, d = x_prompt.shape
    db, nq = x_sample.shape[:2]
    depth, n_pool = cache_mla_ckv.shape[:2]
    npages = page_table.shape[1]
    past = npages * PAGE_ROWS
    win_buf = state_nsa_k_win.shape[2]
    np_tok = nb * seq
    ns_tok = db * nq
    assert nq <= QUERY_PAD and cache_mla_ckv.shape[2] == PAGE_ROWS and seq % QUERY_BLOCK == 0
    hh, gg, dh = MLA_HEADS, NSA_KV_GROUPS, NSA_DH

    x = jnp.concatenate([x_prompt.reshape(np_tok, d), x_sample.reshape(ns_tok, d)], axis=0)
    pos = jnp.concatenate([jnp.tile(jnp.arange(seq), nb), jnp.tile(past + jnp.arange(nq), db)])
    cos_m, sin_m = _rope_tables(pos, MLA_ROPE)
    cos_n, sin_n = _rope_tables(pos, NSA_ROT)

    ns_s = -(-(past + nq) // SEL_BLOCK)
    selw = -(-ns_s // LANE) * LANE
    nseg_s = past // CMP_STRIDE
    c_st = np.arange(nseg_s)[:, None] * CMP_STRIDE
    s_st = np.arange(selw)[None, :] * SEL_BLOCK
    overlap = jnp.asarray(((c_st < s_st + SEL_BLOCK) & (c_st + CMP_BLOCK > s_st)), dtype=BF16)
    pps = min(PAGES_PER_STEP, npages)
    nch = npages // pps
    bpc = pps * PAGE_ROWS // SEL_BLOCK
    expand = jnp.asarray(np.arange(LANE)[:, None] == (np.arange(pps * PAGE_ROWS)[None, :] // SEL_BLOCK), dtype=BF16)

    rows_p = [[] for _ in range(8)]
    rows_s = [[] for _ in range(8)]
    for l in range(depth):
        wq = w_mla_q_up[l].reshape(MLA_Q_RANK, hh, MLA_NOPE + MLA_ROPE)
        wqn = wq[:, :, :MLA_NOPE].reshape(MLA_Q_RANK, hh * MLA_NOPE).astype(BF16)
        wqr = wq[:, :, MLA_NOPE:].reshape(MLA_Q_RANK, hh * MLA_ROPE).astype(BF16)
        wuk = w_mla_k_up[l].reshape(MLA_KV_RANK, hh, MLA_NOPE).transpose(1, 2, 0).astype(BF16)
        wuv = w_mla_v_up[l].reshape(MLA_KV_RANK, hh, MLA_V).transpose(1, 0, 2).astype(BF16)
        wi = w_in[l]
        o_q, o_kv, o_kr = 0, MLA_Q_RANK, MLA_Q_RANK + MLA_KV_RANK
        o_qn = o_kr + MLA_ROPE
        o_kv6 = o_qn + NSA_HEADS * dh
        o_g = o_kv6 + 6 * KV_COLS
        w_in_p = jnp.concatenate([
            wi[:, o_q:o_kr], wi[:, o_qn:o_g], wi[:, o_kr:o_qn], wi[:, o_g:],
            jnp.zeros((d, IN_PAD - wi.shape[1]), wi.dtype)], axis=1).astype(BF16)
        pwk, pwv = nsa_cmp_pos_k[l], nsa_cmp_pos_v[l]
        link = nsa_cmp_lin_k[l].astype(BF16)
        linv = nsa_cmp_lin_v[l].astype(BF16)

        x = _ffn(x, ffn1_norm[l], w_ffn1_gate[l].astype(BF16), w_ffn1_up[l].astype(BF16),
                 w_ffn1_down[l].astype(BF16))
        z = _inproj(x, mix_norm[l], w_in_p)
        ql, qr_raw, ckv = _mlaprep(z, mla_q_norm[l], mla_kv_norm[l], wqn, wqr, wuk)

        qr = _rope(qr_raw.reshape(-1, hh, MLA_ROPE), cos_m, sin_m)
        krope = _rope(z[:, None, COL_KR:COL_KR + MLA_ROPE], cos_m, sin_m)[:, 0]
        qn = _rope(z[:, COL_QN:COL_KC].reshape(-1, NSA_HEADS, dh), cos_n, sin_n)

        def kv_rot(col):
            return _rope(z[:, col:col + KV_COLS].reshape(-1, gg, dh), cos_n, sin_n).reshape(-1, KV_COLS)

        k_cmp, k_sel, k_win = kv_rot(COL_KC), kv_rot(COL_KS), kv_rot(COL_KW)
        v_cmp, v_sel, v_win = (z[:, c0:c0 + KV_COLS] for c0 in (COL_VC, COL_VS, COL_VW))
        ckv_b, krope_b = ckv.astype(BF16), krope.astype(BF16)
        k_sel_b, v_sel_b, k_win_b, v_win_b = (a.astype(BF16) for a in (k_sel, v_sel, k_win, v_win))

        qr_t = qr.transpose(1, 0, 2).astype(BF16)
        qn_t = qn.transpose(1, 0, 2).astype(BF16)
        o_mla_p = _mla_prompt(ql, qr_t, ckv_b, krope_b, wuv, nb, seq)
        kcmp_p, vcmp_p = _compress_prompt(k_cmp, v_cmp, pwk, pwv, link, linv, nb, seq)
        o_nsa_p = _nsa_prompt(qn_t, kcmp_p, vcmp_p, k_sel_b, v_sel_b, k_win_b, v_win_b, z, nb, seq)

        sm = slice(np_tok, None)
        ql_s = _pad_queries(ql[:, sm].transpose(1, 0, 2), db, nq)
        qr_s = _pad_queries(qr[sm], db, nq).astype(BF16)
        qn_s = _pad_queries(qn[sm], db, nq).astype(BF16).reshape(db, gg, NSA_HPG * QUERY_PAD, dh)
        gates = jax.nn.sigmoid(z[sm, COL_G:COL_G + GATE_COLS]).reshape(ns_tok, NSA_HEADS, 3)
        gt_s = _pad_queries(gates, db, nq).reshape(db, gg, NSA_HPG * QUERY_PAD, 3)
        pools = [c[l].reshape(n_pool, PAGE_ROWS, -1) for c in
                 (cache_mla_ckv, cache_mla_krope, cache_nsa_k_cmp, cache_nsa_v_cmp, cache_nsa_k_sel, cache_nsa_v_sel)]
        o_c, sel = _sample_select(page_table, pools[2].reshape(n_pool, -1, dh), pools[3].reshape(n_pool, -1, dh), qn_s, pwk, pwv, link, linv, overlap,
                                  past, nq, ns_s, min(SEL_TOPK, ns_s))
        selc = sel[:, :, :nch * bpc].reshape(db, gg * QUERY_PAD, nch, bpc).transpose(0, 2, 1, 3)
        selc = jnp.pad(selc, ((0, 0), (0, 0), (0, 0), (0, LANE - bpc)))
        sell = jnp.broadcast_to(sel[:, :, nch * bpc:nch * bpc + 1], (db, gg * QUERY_PAD, LANE))
        kwin_full = jnp.concatenate([state_nsa_k_win[l].reshape(db, win_buf, KV_COLS),
                                     k_win[sm].reshape(db, nq, KV_COLS)], axis=1)
        vwin_full = jnp.concatenate([state_nsa_v_win[l].reshape(db, win_buf, KV_COLS),
                                     v_win[sm].reshape(db, nq, KV_COLS)], axis=1)
        wpad = ((0, 0), (0, NEW_PAD - nq), (0, 0))
        o_mla_s, o_nsa_s = _sample_attend(
            page_table, pools[0], pools[1], pools[4], pools[5], ql_s, qr_s, qn_s, selc, sell, o_c, gt_s,
            _pad_new(ckv[sm], db, nq), _pad_new(krope[sm], db, nq), _pad_new(k_sel[sm], db, nq),
            _pad_new(v_sel[sm], db, nq), jnp.pad(kwin_full, wpad).astype(BF16),
            jnp.pad(vwin_full, wpad).astype(BF16), expand, wuv, past, nq, win_buf)

        mix_a = jnp.concatenate([o_mla_p, o_mla_s[:, :nq].reshape(ns_tok, -1)], axis=0)
        mix_b = jnp.concatenate([o_nsa_p, o_nsa_s[:, :nq].reshape(ns_tok, -1)], axis=0)
        wo = w_out[l].astype(BF16)
        x = _outproj(x, mix_a, mix_b, wo[:hh * MLA_V], wo[hh * MLA_V:])
        x = _ffn(x, ffn2_norm[l], w_ffn2_gate[l].astype(BF16), w_ffn2_up[l].astype(BF16),
                 w_ffn2_down[l].astype(BF16), final_g=final_norm if l == depth - 1 else None)

        win_p = min(WINDOW, seq)
        new = (ckv, krope, k_cmp, v_cmp, k_sel, v_sel)
        for i, a in enumerate(new):
            tail = a.shape[1:] if i < 2 else (gg, dh)
            rows_p[i].append(a[:np_tok].reshape((nb, seq) + tail))
            rows_s[i].append(a[np_tok:].reshape((db, nq) + tail))
        rows_p[6].append(k_win[:np_tok].reshape(nb, seq, gg, dh)[:, -win_p:])
        rows_p[7].append(v_win[:np_tok].reshape(nb, seq, gg, dh)[:, -win_p:])
        rows_s[6].append(kwin_full[:, -win_buf:].reshape(db, win_buf, gg, dh))
        rows_s[7].append(vwin_full[:, -win_buf:].reshape(db, win_buf, gg, dh))

    y_prompt = x[:np_tok].reshape(nb, seq, d)
    y_sample = x[np_tok:].reshape(db, nq, d)
    out = [y_prompt, y_sample]
    for i in range(8):
        out += [jnp.stack(rows_p[i]), jnp.stack(rows_s[i])]
    return tuple(out)
```

```python
import functools

import jax
import jax.numpy as jnp
import numpy as np
from jax import lax
from jax.experimental import pallas as pl
from jax.experimental.pallas import tpu as pltpu

MLA_HEADS = 8
MLA_NOPE = 128
MLA_ROPE = 64
MLA_V = 128
MLA_Q_RANK = 512
MLA_KV_RANK = 256
NSA_HEADS = 8
NSA_KV_GROUPS = 2
NSA_HPG = NSA_HEADS // NSA_KV_GROUPS
NSA_DH = 128
CMP_BLOCK = 32
CMP_STRIDE = 16
SEL_BLOCK = 64
SEL_TOPK = 16
SEL_LOCAL = 2
WINDOW = 512
ROPE_THETA = 500000.0
NSA_ROT = NSA_DH // 4
RMS_EPS = 1e-6
QUERY_BLOCK = 128
MLA_SCALE = (MLA_NOPE + MLA_ROPE) ** -0.5
NSA_SCALE = NSA_DH ** -0.5
NEG_INF = -1e30
BIG = 1e6
TINY = 1e-30
KV_COLS = NSA_KV_GROUPS * NSA_DH
GATE_COLS = 3 * NSA_HEADS

COL_Q = 0
COL_KV = COL_Q + MLA_Q_RANK
COL_QN = COL_KV + MLA_KV_RANK
COL_KC = COL_QN + NSA_HEADS * NSA_DH
COL_VC = COL_KC + KV_COLS
COL_KS = COL_VC + KV_COLS
COL_VS = COL_KS + KV_COLS
COL_KW = COL_VS + KV_COLS
COL_VW = COL_KW + KV_COLS
COL_KR = COL_VW + KV_COLS
COL_G = COL_KR + MLA_ROPE
LANE = 128
IN_PAD = -(-(COL_G + GATE_COLS) // (3 * LANE)) * (3 * LANE)

QUERY_PAD = 8
NEW_PAD = 16
PAGE_ROWS = 128
SEL_SHIFT = SEL_BLOCK.bit_length() - 1
NEW_SHIFT = NEW_PAD.bit_length() - 1
PAGES_PER_STEP = 16
VMEM_LIMIT = 56 * 1024 * 1024

F32 = jnp.float32
BF16 = jnp.bfloat16


def _dot(a, b):
    return jnp.dot(a, b, preferred_element_type=F32)


def _dot_nt(a, b):
    return lax.dot_general(a, b, (((1,), (1,)), ((), ())), preferred_element_type=F32)


def _rms(x, g):
    return x * lax.rsqrt(jnp.mean(x * x, axis=-1, keepdims=True) + RMS_EPS) * g


def _split3_dot_nt(w_bf16, x):
    hi = x.astype(BF16)
    r1 = x - hi.astype(F32)
    mid = r1.astype(BF16)
    lo = (r1 - mid.astype(F32)).astype(BF16)
    return _dot_nt(w_bf16, hi) + _dot_nt(w_bf16, mid) + _dot_nt(w_bf16, lo)


def _split3_dot(x, w_bf16):
    hi = x.astype(BF16)
    r1 = x - hi.astype(F32)
    mid = r1.astype(BF16)
    lo = (r1 - mid.astype(F32)).astype(BF16)
    return _dot(hi, w_bf16) + _dot(mid, w_bf16) + _dot(lo, w_bf16)


def _row_tile(n, want):
    t = min(want, n)
    while n % t:
        t -= 8
    return t


def _params(*sem):
    return pltpu.CompilerParams(dimension_semantics=sem, vmem_limit_bytes=VMEM_LIMIT)


def _ffn_kernel(x_ref, g_ref, wg_ref, wu_ref, wd_ref, *rest, final):
    if final:
        fg_ref, o_ref, h_ref, acc_ref = rest
    else:
        o_ref, h_ref, acc_ref = rest
    j = pl.program_id(1)

    @pl.when(j == 0)
    def _():
        h_ref[...] = _rms(x_ref[...], g_ref[...]).astype(BF16)
        acc_ref[...] = jnp.zeros_like(acc_ref)

    h = h_ref[...]
    a = _dot(h, wg_ref[...])
    u = _dot(h, wu_ref[...])
    act = (a / (1.0 + jnp.exp(-a))) * u
    acc_ref[...] += _dot(act.astype(BF16), wd_ref[...])

    @pl.when(j == pl.num_programs(1) - 1)
    def _():
        y = x_ref[...] + 0.5 * acc_ref[...]
        if final:
            y = _rms(y, fg_ref[...])
        o_ref[...] = y


def _ffn(x, g, wg, wu, wd, final_g=None):
    n, d = x.shape
    dff = wg.shape[1]
    tm = _row_tile(n, 512)
    tf = 512 if dff % 512 == 0 else dff
    in_specs = [
        pl.BlockSpec((tm, d), lambda i, j: (i, 0)),
        pl.BlockSpec((1, d), lambda i, j: (0, 0)),
        pl.BlockSpec((d, tf), lambda i, j: (0, j)),
        pl.BlockSpec((d, tf), lambda i, j: (0, j)),
        pl.BlockSpec((tf, d), lambda i, j: (j, 0)),
    ]
    args = [x, g.reshape(1, d), wg, wu, wd]
    if final_g is not None:
        in_specs.append(pl.BlockSpec((1, d), lambda i, j: (0, 0)))
        args.append(final_g.reshape(1, d))
    return pl.pallas_call(
        functools.partial(_ffn_kernel, final=final_g is not None),
        grid=(n // tm, dff // tf),
        in_specs=in_specs,
        out_specs=pl.BlockSpec((tm, d), lambda i, j: (i, 0)),
        out_shape=jax.ShapeDtypeStruct((n, d), F32),
        scratch_shapes=[pltpu.VMEM((tm, d), BF16), pltpu.VMEM((tm, d), F32)],
        compiler_params=_params("parallel", "arbitrary"),
        name="ffn",
    )(*args)


def _inproj_kernel(x_ref, g_ref, w_ref, o_ref, h_ref):
    @pl.when(pl.program_id(1) == 0)
    def _():
        h_ref[...] = _rms(x_ref[...], g_ref[...]).astype(BF16)

    o_ref[...] = _dot(h_ref[...], w_ref[...])


def _inproj(x, g, w):
    n, d = x.shape
    nout = w.shape[1]
    tm = _row_tile(n, 512)
    tn = nout // 3
    return pl.pallas_call(
        _inproj_kernel,
        grid=(n // tm, nout // tn),
        in_specs=[
            pl.BlockSpec((tm, d), lambda i, j: (i, 0)),
            pl.BlockSpec((1, d), lambda i, j: (0, 0)),
            pl.BlockSpec((d, tn), lambda i, j: (0, j)),
        ],
        out_specs=pl.BlockSpec((tm, tn), lambda i, j: (i, j)),
        out_shape=jax.ShapeDtypeStruct((n, nout), F32),
        scratch_shapes=[pltpu.VMEM((tm, d), BF16)],
        compiler_params=_params("parallel", "arbitrary"),
        name="inproj",
    )(x, g.reshape(1, d), w)


def _mlaprep_kernel(zq_ref, zkv_ref, gq_ref, gkv_ref, wqn_ref, wqr_ref, wuk_ref, ql_ref, qr_ref, ckv_ref):
    cq = _rms(zq_ref[...], gq_ref[...]).astype(BF16)
    qn = _dot(cq, wqn_ref[...]).astype(BF16)
    for h in range(MLA_HEADS):
        ql_ref[h] = _dot(qn[:, h * MLA_NOPE:(h + 1) * MLA_NOPE], wuk_ref[h]).astype(BF16)
    qr_ref[...] = _dot(cq, wqr_ref[...])
    ckv_ref[...] = _rms(zkv_ref[...], gkv_ref[...])


def _mlaprep(z, gq, gkv, wqn, wqr, wuk):
    n = z.shape[0]
    tm = _row_tile(n, 512)
    return pl.pallas_call(
        _mlaprep_kernel,
        grid=(n // tm,),
        in_specs=[
            pl.BlockSpec((tm, MLA_Q_RANK), lambda i: (i, COL_Q // MLA_Q_RANK)),
            pl.BlockSpec((tm, MLA_KV_RANK), lambda i: (i, COL_KV // MLA_KV_RANK)),
            pl.BlockSpec((1, MLA_Q_RANK), lambda i: (0, 0)),
            pl.BlockSpec((1, MLA_KV_RANK), lambda i: (0, 0)),
            pl.BlockSpec(wqn.shape, lambda i: (0, 0)),
            pl.BlockSpec(wqr.shape, lambda i: (0, 0)),
            pl.BlockSpec(wuk.shape, lambda i: (0, 0, 0)),
        ],
        out_specs=[
            pl.BlockSpec((MLA_HEADS, tm, MLA_KV_RANK), lambda i: (0, i, 0)),
            pl.BlockSpec((tm, MLA_HEADS * MLA_ROPE), lambda i: (i, 0)),
            pl.BlockSpec((tm, MLA_KV_RANK), lambda i: (i, 0)),
        ],
        out_shape=[
            jax.ShapeDtypeStruct((MLA_HEADS, n, MLA_KV_RANK), BF16),
            jax.ShapeDtypeStruct((n, MLA_HEADS * MLA_ROPE), F32),
            jax.ShapeDtypeStruct((n, MLA_KV_RANK), F32),
        ],
        compiler_params=_params("parallel"),
        name="mlaprep",
    )(z, z, gq.reshape(1, -1), gkv.reshape(1, -1), wqn, wqr, wuk)


def _mla_prompt_kernel(ql_ref, qr_ref, ckv_ref, kr_ref, wuv_ref, o_ref, m_ref, l_ref, acc_ref, *, tq, kc):
    q0 = pl.program_id(1) * tq
    hh = MLA_HEADS
    ql = ql_ref[...].reshape(hh * tq, MLA_KV_RANK)
    qr = qr_ref[...].reshape(hh * tq, MLA_ROPE)
    m_ref[...] = jnp.full(m_ref.shape, NEG_INF, F32)
    l_ref[...] = jnp.zeros_like(l_ref)
    acc_ref[...] = jnp.zeros_like(acc_ref)
    t = q0 + lax.broadcasted_iota(jnp.int32, (1, tq, kc), 1)
    koff = lax.broadcasted_iota(jnp.int32, (1, tq, kc), 2)

    def body(c, carry):
        k0 = pl.multiple_of(c * kc, kc)
        kk = ckv_ref[pl.ds(k0, kc), :]
        kr = kr_ref[pl.ds(k0, kc), :]
        s = (_dot_nt(ql, kk) + _dot_nt(qr, kr)) * MLA_SCALE
        s = jnp.where(koff + k0 <= t, s.reshape(hh, tq, kc), NEG_INF)
        m_prev = m_ref[...]
        m_new = jnp.maximum(m_prev, jnp.max(s, axis=-1, keepdims=True))
        alpha = jnp.exp(m_prev - m_new)
        p = jnp.exp(s - m_new)
        l_ref[...] = alpha * l_ref[...] + jnp.sum(p, axis=-1, keepdims=True)
        pv = _dot(p.reshape(hh * tq, kc).astype(BF16), kk)
        acc_ref[...] = alpha * acc_ref[...] + pv.reshape(hh, tq, MLA_KV_RANK)
        m_ref[...] = m_new
        return carry

    lax.fori_loop(0, (q0 + tq + kc - 1) // kc, body, 0)
    o = (acc_ref[...] / l_ref[...]).astype(BF16)
    for h in range(hh):
        o_ref[:, h * MLA_V:(h + 1) * MLA_V] = _dot(o[h], wuv_ref[h]).astype(o_ref.dtype)


def _mla_prompt(ql, qr, ckv, kr, wuv, nb, seq):
    tq = min(QUERY_BLOCK, seq)
    kc = min(512, seq)
    nq = seq // tq
    hh = MLA_HEADS
    return pl.pallas_call(
        functools.partial(_mla_prompt_kernel, tq=tq, kc=kc),
        grid=(nb, nq),
        in_specs=[
            pl.BlockSpec((hh, tq, MLA_KV_RANK), lambda b, i: (0, b * nq + i, 0)),
            pl.BlockSpec((hh, tq, MLA_ROPE), lambda b, i: (0, b * nq + i, 0)),
            pl.BlockSpec((seq, MLA_KV_RANK), lambda b, i: (b, 0)),
            pl.BlockSpec((seq, MLA_ROPE), lambda b, i: (b, 0)),
            pl.BlockSpec(wuv.shape, lambda b, i: (0, 0, 0)),
        ],
        out_specs=pl.BlockSpec((tq, hh * MLA_V), lambda b, i: (b * nq + i, 0)),
        out_shape=jax.ShapeDtypeStruct((nb * seq, hh * MLA_V), BF16),
        scratch_shapes=[
            pltpu.VMEM((hh, tq, 1), F32),
            pltpu.VMEM((hh, tq, 1), F32),
            pltpu.VMEM((hh, tq, MLA_KV_RANK), F32),
        ],
        compiler_params=_params("parallel", "arbitrary"),
        name="mla_prompt",
    )(ql, qr, ckv, kr, wuv)


SUBLANES = 8
TILES_PER_SEG = NSA_KV_GROUPS * CMP_STRIDE // SUBLANES
SEGS_PER_PAGE = PAGE_ROWS // CMP_STRIDE
PAGE_IROWS = NSA_KV_GROUPS * PAGE_ROWS


def _page_segment_sums(x_ref, row0, wt_ref):
    sub = lax.broadcasted_iota(jnp.int32, (SUBLANES, NSA_DH), 0)
    out = [[jnp.zeros((SEGS_PER_PAGE, NSA_DH), F32) for _ in range(NSA_KV_GROUPS)] for _ in range(2)]
    for n in range(SEGS_PER_PAGE):
        tiles = [x_ref[pl.ds(row0 + (n * TILES_PER_SEG + k) * SUBLANES, SUBLANES), :] for k in range(TILES_PER_SEG)]
        for m in range(2):
            w0 = m * TILES_PER_SEG * SUBLANES
            p = tiles[0] * wt_ref[w0:w0 + SUBLANES, :]
            for k in range(1, TILES_PER_SEG):
                p = p + tiles[k] * wt_ref[w0 + k * SUBLANES:w0 + (k + 1) * SUBLANES, :]
            p = p + pltpu.roll(p, 4, 0)
            p = p + pltpu.roll(p, 2, 0)
            q = pltpu.roll(p, 1, 0)
            for g in range(NSA_KV_GROUPS):
                out[m][g] = jnp.where(sub == n, p if n % 2 == g else q, out[m][g])
    return out


def _compress_prompt_kernel(k_ref, v_ref, wtk_ref, wtv_ref, link_ref, linv_ref, kc_ref, vc_ref, *, nseg):
    npg = nseg // SEGS_PER_PAGE
    for x_ref, wt_ref, lin_ref, o_ref in ((k_ref, wtk_ref, link_ref, kc_ref), (v_ref, wtv_ref, linv_ref, vc_ref)):
        parts = [_page_segment_sums(x_ref, pg * PAGE_IROWS, wt_ref) for pg in range(npg)]
        for g in range(NSA_KV_GROUPS):
            a0 = jnp.concatenate([pt[0][g] for pt in parts], axis=0)
            a1 = jnp.concatenate([pt[1][g] for pt in parts], axis=0)
            acc = (a0 + pltpu.roll(a1, nseg - 1, 0)).astype(BF16)
            o_ref[g] = _dot(acc, lin_ref[...]).astype(BF16)


def _compress_prompt(k, v, pwk, pwv, link, linv, nb, seq):
    nseg = seq // CMP_STRIDE
    row = pl.BlockSpec((NSA_KV_GROUPS * seq, NSA_DH), lambda b: (b, 0))
    pw = pl.BlockSpec((NSA_KV_GROUPS * CMP_BLOCK, NSA_DH), lambda b: (0, 0))
    lin = pl.BlockSpec((NSA_DH, NSA_DH), lambda b: (0, 0))
    out = pl.BlockSpec((None, NSA_KV_GROUPS, nseg, NSA_DH), lambda b: (b, 0, 0, 0))
    shp = jax.ShapeDtypeStruct((nb, NSA_KV_GROUPS, nseg, NSA_DH), BF16)
    return pl.pallas_call(
        functools.partial(_compress_prompt_kernel, nseg=nseg),
        grid=(nb,),
        in_specs=[row, row, pw, pw, lin, lin],
        out_specs=[out, out],
        out_shape=[shp, shp],
        compiler_params=_params("parallel"),
        name="compress_prompt",
    )(k.reshape(-1, NSA_DH), v.reshape(-1, NSA_DH), pwk, pwv, link, linv)


def _masked_softmax(s, mask):
    s = jnp.where(mask, s, NEG_INF)
    e = jnp.where(mask, jnp.exp(s - jnp.max(s, axis=-1, keepdims=True)), 0.0)
    return e / jnp.maximum(jnp.sum(e, axis=-1, keepdims=True), TINY)


def _forced_importance(imp, blk, cur, ns):
    valid = blk <= cur
    forced = valid & ((blk == 0) | (blk > cur - SEL_LOCAL))
    imp = jnp.where(forced, BIG, jnp.where(valid, imp, -BIG))
    return jnp.where(blk < ns, imp, -2.0 * BIG)


def _nsa_prompt_kernel(qn_ref, kcmp_ref, vcmp_ref, ks_ref, vs_ref, kw_ref, vw_ref, zg_ref, o_ref,
                       m_ref, l_ref, acc_ref, *, tq, seq, kc, ntop):
    g = pl.program_id(1)
    q0 = pl.program_id(2) * tq
    jj = NSA_HPG
    nseg = seq // CMP_STRIDE
    ns = seq // SEL_BLOCK
    q = qn_ref[...].reshape(jj * tq, NSA_DH)

    t3 = q0 + lax.broadcasted_iota(jnp.int32, (1, tq, nseg), 1)
    cmp_last = lax.broadcasted_iota(jnp.int32, (1, tq, nseg), 2) * CMP_STRIDE + (CMP_BLOCK - 1)
    sc = (_dot_nt(q, kcmp_ref[...]) * NSA_SCALE).reshape(jj, tq, nseg)
    pc = _masked_softmax(sc, cmp_last <= t3)
    o_c = _dot(pc.reshape(jj * tq, nseg).astype(BF16), vcmp_ref[...])

    psum = jnp.sum(pc, axis=0)
    blk_r = lax.broadcasted_iota(jnp.int32, (ns, nseg), 0) * SEL_BLOCK
    c_st = lax.broadcasted_iota(jnp.int32, (ns, nseg), 1) * CMP_STRIDE
    ov_t = jnp.where((c_st < blk_r + SEL_BLOCK) & (c_st + CMP_BLOCK > blk_r), 1.0, 0.0).astype(BF16)
    imp_t = _split3_dot_nt(ov_t, psum)
    blk = lax.broadcasted_iota(jnp.int32, (ns, tq), 0)
    cur = (q0 + lax.broadcasted_iota(jnp.int32, (ns, tq), 1)) >> SEL_SHIFT
    imp_t = _forced_importance(imp_t, blk, cur, ns)
    rank = jnp.zeros((ns, tq), F32)
    for mm in range(ns):
        row = imp_t[mm:mm + 1, :]
        tie = jnp.where(blk > mm, 1.0, 0.0)
        rank = rank + jnp.where(row > imp_t, 1.0, jnp.where(row == imp_t, tie, 0.0))
    sel_t = jnp.where(rank < ntop, 1.0, 0.0)
    sel = jnp.concatenate([sel_t, jnp.zeros((LANE - ns, tq), F32)], axis=0).T.astype(BF16)

    m_ref[...] = jnp.full(m_ref.shape, NEG_INF, F32)
    l_ref[...] = jnp.zeros_like(l_ref)
    acc_ref[...] = jnp.zeros_like(acc_ref)
    tk = q0 + lax.broadcasted_iota(jnp.int32, (1, tq, kc), 1)
    koff = lax.broadcasted_iota(jnp.int32, (1, tq, kc), 2)
    e_row = lax.broadcasted_iota(jnp.int32, (LANE, kc), 0)
    e_col = lax.broadcasted_iota(jnp.int32, (LANE, kc), 1) >> SEL_SHIFT

    def body(c, carry):
        k0 = pl.multiple_of(c * kc, kc)
        kk = ks_ref[pl.ds(k0, kc), :]
        vv = vs_ref[pl.ds(k0, kc), :]
        expand = jnp.where(e_row == e_col + c * (kc // SEL_BLOCK), 1.0, 0.0).astype(BF16)
        chosen = _dot(sel, expand).reshape(1, tq, kc)
        s = (_dot_nt(q, kk) * NSA_SCALE).reshape(jj, tq, kc)
        s = jnp.where((chosen > 0.5) & (koff + k0 <= tk), s, NEG_INF)
        m_prev = m_ref[...]
        m_new = jnp.maximum(m_prev, jnp.max(s, axis=-1, keepdims=True))
        alpha = jnp.exp(m_prev - m_new)
        p = jnp.exp(s - m_new)
        l_ref[...] = alpha * l_ref[...] + jnp.sum(p, axis=-1, keepdims=True)
        pv = _dot(p.reshape(jj * tq, kc).astype(BF16), vv)
        acc_ref[...] = alpha * acc_ref[...] + pv.reshape(jj, tq, NSA_DH)
        m_ref[...] = m_new
        return carry

    lax.fori_loop(0, (q0 + tq + kc - 1) // kc, body, 0)
    o_s = acc_ref[...] / l_ref[...]

    wl = min(WINDOW + tq, seq)
    w0 = pl.multiple_of(jnp.clip(q0 - WINDOW, 0, seq - wl), tq)
    kw = kw_ref[pl.ds(w0, wl), :]
    vw = vw_ref[pl.ds(w0, wl), :]
    tw = q0 + lax.broadcasted_iota(jnp.int32, (1, tq, wl), 1)
    wpos = w0 + lax.broadcasted_iota(jnp.int32, (1, tq, wl), 2)
    sw = (_dot_nt(q, kw) * NSA_SCALE).reshape(jj, tq, wl)
    sw = jnp.where((wpos >= tw - WINDOW) & (wpos <= tw), sw, NEG_INF)
    pw = jnp.exp(sw - jnp.max(sw, axis=-1, keepdims=True))
    pw = pw / jnp.sum(pw, axis=-1, keepdims=True)
    o_w = _dot(pw.reshape(jj * tq, wl).astype(BF16), vw).reshape(jj, tq, NSA_DH)

    o_c = o_c.reshape(jj, tq, NSA_DH)
    zg = zg_ref[...]
    lane = lax.broadcasted_iota(jnp.int32, zg.shape, 1)
    for j in range(jj):
        def gate(k, j=j):
            col = jnp.sum(jnp.where(lane == COL_G % LANE + (g * jj + j) * 3 + k, zg, 0.0), axis=-1, keepdims=True)
            return 1.0 / (1.0 + jnp.exp(-col))
        o = gate(0) * o_c[j] + gate(1) * o_s[j] + gate(2) * o_w[j]
        o_ref[:, j * NSA_DH:(j + 1) * NSA_DH] = o.astype(o_ref.dtype)


def _nsa_prompt(qn, kcmp, vcmp, ks, vs, kw, vw, z, nb, seq):
    tq = min(QUERY_BLOCK, seq)
    kc = min(512, seq)
    nq = seq // tq
    jj = NSA_HPG
    nseg = seq // CMP_STRIDE
    ntop = min(SEL_TOPK, seq // SEL_BLOCK)
    rows = pl.BlockSpec((seq, NSA_DH), lambda b, g, i: (b, g))
    cmp = pl.BlockSpec((None, None, nseg, NSA_DH), lambda b, g, i: (b, g, 0, 0))
    return pl.pallas_call(
        functools.partial(_nsa_prompt_kernel, tq=tq, seq=seq, kc=kc, ntop=ntop),
        grid=(nb, NSA_KV_GROUPS, nq),
        in_specs=[
            pl.BlockSpec((jj, tq, NSA_DH), lambda b, g, i: (g, b * nq + i, 0)),
            cmp, cmp, rows, rows, rows, rows,
            pl.BlockSpec((tq, LANE), lambda b, g, i: (b * nq + i, COL_G // LANE)),
        ],
        out_specs=pl.BlockSpec((tq, jj * NSA_DH), lambda b, g, i: (b * nq + i, g)),
        out_shape=jax.ShapeDtypeStruct((nb * seq, NSA_HEADS * NSA_DH), BF16),
        scratch_shapes=[
            pltpu.VMEM((jj, tq, 1), F32),
            pltpu.VMEM((jj, tq, 1), F32),
            pltpu.VMEM((jj, tq, NSA_DH), F32),
        ],
        compiler_params=_params("parallel", "parallel", "arbitrary"),
        name="nsa_prompt",
    )(qn, kcmp, vcmp, ks, vs, kw, vw, z)


def _sample_select_kernel(pt_ref, *refs, pps, nch, past, nq, ns, ntop):
    del pt_ref
    kp = refs[:pps]
    vp = refs[pps:2 * pps]
    (qn_ref, pwk_ref, pwv_ref, link_ref, linv_ref, ov_ref, oc_ref, sel_ref,
     a0k_ref, a1k_ref, a0v_ref, a1v_ref) = refs[2 * pps:]
    c = pl.program_id(1)
    segs = SEGS_PER_PAGE
    nseg = nch * pps * segs
    for p in range(pps):
        r0 = pl.multiple_of(c * (pps * segs) + p * segs, segs)
        for pg, pw_ref, a0_ref, a1_ref in ((kp[p], pwk_ref, a0k_ref, a1k_ref), (vp[p], pwv_ref, a0v_ref, a1v_ref)):
            part = _page_segment_sums(pg, 0, pw_ref)
            for g in range(NSA_KV_GROUPS):
                a0_ref[g, pl.ds(r0, segs), :] = part[0][g]
                a1_ref[g, pl.ds(r0, segs), :] = part[1][g]

    @pl.when(c == nch - 1)
    def _():
        qp = QUERY_PAD
        rows = NSA_HPG * qp
        qi = lax.broadcasted_iota(jnp.int32, (1, qp, nseg), 1)
        t3 = past + jnp.minimum(qi, nq - 1)
        cmp_last = lax.broadcasted_iota(jnp.int32, (1, qp, nseg), 2) * CMP_STRIDE + (CMP_BLOCK - 1)
        imps = []
        for g in range(NSA_KV_GROUPS):
            acc_k = (a0k_ref[g] + pltpu.roll(a1k_ref[g], nseg - 1, 0)).astype(BF16)
            acc_v = (a0v_ref[g] + pltpu.roll(a1v_ref[g], nseg - 1, 0)).astype(BF16)
            kc = _dot(acc_k, link_ref[...]).astype(BF16)
            vc = _dot(acc_v, linv_ref[...]).astype(BF16)
            sc = (_dot_nt(qn_ref[g], kc) * NSA_SCALE).reshape(NSA_HPG, qp, nseg)
            pc = _masked_softmax(sc, cmp_last <= t3)
            oc_ref[g] = _dot(pc.reshape(rows, nseg).astype(BF16), vc)
            imps.append(_split3_dot(jnp.sum(pc, axis=0), ov_ref[...]))
        imp = jnp.concatenate(imps, axis=0)
        selw = imp.shape[1]
        blk = lax.broadcasted_iota(jnp.int32, imp.shape, 1)
        qrow = lax.broadcasted_iota(jnp.int32, imp.shape, 0) & (qp - 1)
        cur = (past + jnp.minimum(qrow, nq - 1)) >> SEL_SHIFT
        work = _forced_importance(imp, blk, cur, ns)
        blk_f = blk.astype(F32)
        chosen = jnp.zeros(imp.shape, F32)
        for _ in range(ntop):
            top = jnp.max(work, axis=-1, keepdims=True)
            first = jnp.min(jnp.where(work == top, blk_f, float(selw)), axis=-1, keepdims=True)
            hit = blk_f == first
            chosen = jnp.where(hit, 1.0, chosen)
            work = jnp.where(hit, -4.0 * BIG, work)
        sel_ref[...] = chosen


def _sample_select(page_table, kpool, vpool, qn, pwk, pwv, link, linv, ov, past, nq, ns, ntop):
    db, npages = page_table.shape
    pps = min(PAGES_PER_STEP, npages)
    nch = npages // pps
    nseg = npages * (PAGE_ROWS // CMP_STRIDE)
    rows = NSA_HPG * QUERY_PAD
    selw = ov.shape[1]
    pages = [pl.BlockSpec((None, NSA_KV_GROUPS * PAGE_ROWS, NSA_DH),
                          lambda b, c, pt, p=p: (pt[b * npages + c * pps + p], 0, 0)) for p in range(pps)]
    const2 = lambda b, c, pt: (0, 0)
    grid_spec = pltpu.PrefetchScalarGridSpec(
        num_scalar_prefetch=1,
        grid=(db, nch),
        in_specs=pages + pages + [
            pl.BlockSpec((None, NSA_KV_GROUPS, rows, NSA_DH), lambda b, c, pt: (b, 0, 0, 0)),
            pl.BlockSpec((NSA_KV_GROUPS * CMP_BLOCK, NSA_DH), const2),
            pl.BlockSpec((NSA_KV_GROUPS * CMP_BLOCK, NSA_DH), const2),
            pl.BlockSpec((NSA_DH, NSA_DH), const2),
            pl.BlockSpec((NSA_DH, NSA_DH), const2),
            pl.BlockSpec(ov.shape, const2),
        ],
        out_specs=[
            pl.BlockSpec((None, NSA_KV_GROUPS, rows, NSA_DH), lambda b, c, pt: (b, 0, 0, 0)),
            pl.BlockSpec((None, NSA_KV_GROUPS * QUERY_PAD, selw), lambda b, c, pt: (b, 0, 0)),
        ],
        scratch_shapes=[pltpu.VMEM((NSA_KV_GROUPS, nseg, NSA_DH), F32) for _ in range(4)],
    )
    return pl.pallas_call(
        functools.partial(_sample_select_kernel, pps=pps, nch=nch, past=past, nq=nq, ns=ns, ntop=ntop),
        grid_spec=grid_spec,
        out_shape=[
            jax.ShapeDtypeStruct((db, NSA_KV_GROUPS, rows, NSA_DH), F32),
            jax.ShapeDtypeStruct((db, NSA_KV_GROUPS * QUERY_PAD, selw), F32),
        ],
        compiler_params=_params("parallel", "arbitrary"),
        name="sample_select",
    )(page_table.reshape(-1), *([kpool] * pps), *([vpool] * pps), qn, pwk, pwv, link, linv, ov)


def _sample_attend_kernel(pt_ref, *refs, pps, nch, past, nq, win_buf):
    del pt_ref
    ckv_p = refs[:pps]
    kr_p = refs[pps:2 * pps]
    ks_p = refs[2 * pps:3 * pps]
    vs_p = refs[3 * pps:4 * pps]
    (ql_ref, qr_ref, qn_ref, selc_ref, sell_ref, oc_ref, gt_ref, ckvn_ref, krn_ref, ksn_ref, vsn_ref,
     kw_ref, vw_ref, e_ref, wuv_ref, omla_ref, onsa_ref,
     kc_s, kr_s, ks_s, vs_s, m1_ref, l1_ref, acc1_ref, m2_ref, l2_ref, acc2_ref) = refs[4 * pps:]
    c = pl.program_id(1)
    qp = QUERY_PAD
    jj = NSA_HPG
    hh = MLA_HEADS
    kk = pps * PAGE_ROWS

    @pl.when(c == 0)
    def _():
        m1_ref[...] = jnp.full(m1_ref.shape, NEG_INF, F32)
        l1_ref[...] = jnp.zeros_like(l1_ref)
        acc1_ref[...] = jnp.zeros_like(acc1_ref)
        m2_ref[...] = jnp.full(m2_ref.shape, NEG_INF, F32)
        l2_ref[...] = jnp.zeros_like(l2_ref)
        acc2_ref[...] = jnp.zeros_like(acc2_ref)

    for p in range(pps):
        sl = slice(p * PAGE_ROWS, (p + 1) * PAGE_ROWS)
        sl2 = slice(p * PAGE_IROWS, (p + 1) * PAGE_IROWS)
        kc_s[sl, :] = ckv_p[p][...].astype(BF16)
        kr_s[:, sl] = kr_p[p][...].astype(BF16)
        ks_s[sl2, :] = ks_p[p][...].astype(BF16)
        vs_s[sl2, :] = vs_p[p][...].astype(BF16)

    def online(m_ref, l_ref, acc_ref, s, v):
        m_prev = m_ref[...]
        m_new = jnp.maximum(m_prev, jnp.max(s, axis=-1, keepdims=True))
        alpha = jnp.exp(m_prev - m_new)
        p = jnp.exp(s - m_new)
        l_ref[...] = alpha * l_ref[...] + jnp.sum(p, axis=-1, keepdims=True)
        acc_ref[...] = alpha * acc_ref[...] + _dot(p.astype(BF16), v)
        m_ref[...] = m_new

    ql = ql_ref[...]
    qr = qr_ref[...]
    kc = kc_s[...]
    s1 = (_dot_nt(ql, kc) + _dot(qr, kr_s[...])) * MLA_SCALE
    online(m1_ref, l1_ref, acc1_ref, s1, kc)

    gg = NSA_KV_GROUPS
    qn = qn_ref[...]
    chosen = _dot(selc_ref[...].astype(BF16), e_ref[...])
    s2 = (_dot_nt(qn, ks_s[...]) * NSA_SCALE).reshape(gg, jj, qp, gg * kk)
    s2 = jnp.where(chosen.reshape(gg, 1, qp, gg * kk) > 0.5, s2, NEG_INF).reshape(gg * jj * qp, gg * kk)
    online(m2_ref, l2_ref, acc2_ref, s2, vs_s[...])

    @pl.when(c == nch - 1)
    def _():
        npad = NEW_PAD
        kidx = lax.broadcasted_iota(jnp.int32, (1, qp, npad), 2)
        qidx = lax.broadcasted_iota(jnp.int32, (1, qp, npad), 1)
        new_ok = (kidx <= qidx) & (kidx < nq)
        ckvn = ckvn_ref[...]
        s1n = (_dot_nt(ql, ckvn) + _dot_nt(qr, krn_ref[...])) * MLA_SCALE
        s1n = jnp.where(new_ok, s1n.reshape(hh, qp, npad), NEG_INF).reshape(hh * qp, npad)
        online(m1_ref, l1_ref, acc1_ref, s1n, ckvn)
        o_lat = (acc1_ref[...] / l1_ref[...]).astype(BF16)
        for h in range(hh):
            omla_ref[:, h * MLA_V:(h + 1) * MLA_V] = _dot(o_lat[h * qp:(h + 1) * qp], wuv_ref[h]).astype(omla_ref.dtype)

        wl = kw_ref.shape[0]
        widx = lax.broadcasted_iota(jnp.int32, (1, qp, wl), 2)
        wpos = past - win_buf + widx
        tw = past + jnp.minimum(lax.broadcasted_iota(jnp.int32, (1, qp, wl), 1), nq - 1)
        win_ok = (widx < win_buf + nq) & (wpos >= tw - WINDOW) & (wpos <= tw)
        shp = (gg, 1, qp, gg * npad)
        ncol = lax.broadcasted_iota(jnp.int32, shp, 3)
        nrow_g = lax.broadcasted_iota(jnp.int32, shp, 0)
        nq_i = lax.broadcasted_iota(jnp.int32, shp, 2)
        nkk = ncol & (npad - 1)
        sel_new = (((ncol >> NEW_SHIFT) == nrow_g) & (nkk <= nq_i) & (nkk < nq)
                   & (sell_ref[:, 0:gg * npad].reshape(shp) > 0.5))
        s2n = (_dot_nt(qn, ksn_ref[...]) * NSA_SCALE).reshape(gg, jj, qp, gg * npad)
        s2n = jnp.where(sel_new, s2n, NEG_INF).reshape(gg * jj * qp, gg * npad)
        online(m2_ref, l2_ref, acc2_ref, s2n, vsn_ref[...])
        o_s_all = acc2_ref[...] / l2_ref[...]
        for g in range(NSA_KV_GROUPS):
            cols = slice(g * NSA_DH, (g + 1) * NSA_DH)
            rws = slice(g * jj * qp, (g + 1) * jj * qp)
            qg = qn[rws]
            o_s = o_s_all[rws]

            sw = (_dot_nt(qg, kw_ref[:, cols]) * NSA_SCALE).reshape(jj, qp, wl)
            sw = jnp.where(win_ok, sw, NEG_INF).reshape(jj * qp, wl)
            pw = jnp.exp(sw - jnp.max(sw, axis=-1, keepdims=True))
            pw = pw / jnp.sum(pw, axis=-1, keepdims=True)
            o_w = _dot(pw.astype(BF16), vw_ref[:, cols])

            gt = gt_ref[rws, :]
            o = gt[:, 0:1] * oc_ref[rws, :] + gt[:, 1:2] * o_s + gt[:, 2:3] * o_w
            for j in range(jj):
                hcol = (g * jj + j) * NSA_DH
                onsa_ref[:, hcol:hcol + NSA_DH] = o[j * qp:(j + 1) * qp].astype(onsa_ref.dtype)


def _sample_attend(page_table, ckv_pool, kr_pool, ks_pool, vs_pool, ql, qr, qn, selc, sell, oc, gt,
                   ckvn, krn, ksn, vsn, kw, vw, expand, wuv, past, nq, win_buf):
    db, npages = page_table.shape
    pps = min(PAGES_PER_STEP, npages)
    nch = npages // pps
    kk = pps * PAGE_ROWS
    qp = QUERY_PAD
    hh = MLA_HEADS
    rows = NSA_HPG * qp

    def pages(rows_, width):
        return [pl.BlockSpec((None, rows_, width),
                             lambda b, c, pt, p=p: (pt[b * npages + c * pps + p], 0, 0)) for p in range(pps)]

    def per_seq(shape):
        nd = len(shape)
        return pl.BlockSpec((None,) + tuple(shape), lambda b, c, pt: (b,) + (0,) * nd)

    def const(shape):
        nd = len(shape)
        return pl.BlockSpec(tuple(shape), lambda b, c, pt: (0,) * nd)

    in_specs = (
        pages(PAGE_ROWS, MLA_KV_RANK) + pages(MLA_ROPE, PAGE_ROWS)
        + pages(PAGE_IROWS, NSA_DH) + pages(PAGE_IROWS, NSA_DH) + [
            per_seq((hh * qp, MLA_KV_RANK)),
            per_seq((hh * qp, MLA_ROPE)),
            per_seq((NSA_KV_GROUPS * rows, NSA_DH)),
            pl.BlockSpec((None, None, NSA_KV_GROUPS * qp, LANE), lambda b, c, pt: (b, c, 0, 0)),
            per_seq((NSA_KV_GROUPS * qp, LANE)),
            per_seq((NSA_KV_GROUPS * rows, NSA_DH)),
            per_seq((NSA_KV_GROUPS * rows, 3)),
            per_seq((NEW_PAD, MLA_KV_RANK)),
            per_seq((NEW_PAD, MLA_ROPE)),
            per_seq((NSA_KV_GROUPS * NEW_PAD, NSA_DH)),
            per_seq((NSA_KV_GROUPS * NEW_PAD, NSA_DH)),
            per_seq(kw.shape[1:]),
            per_seq(vw.shape[1:]),
            const(expand.shape),
            const(wuv.shape),
        ])
    grid_spec = pltpu.PrefetchScalarGridSpec(
        num_scalar_prefetch=1,
        grid=(db, nch),
        in_specs=in_specs,
        out_specs=[per_seq((qp, hh * MLA_V)), per_seq((qp, NSA_HEADS * NSA_DH))],
        scratch_shapes=[
            pltpu.VMEM((kk, MLA_KV_RANK), BF16),
            pltpu.VMEM((MLA_ROPE, kk), BF16),
            pltpu.VMEM((NSA_KV_GROUPS * kk, NSA_DH), BF16),
            pltpu.VMEM((NSA_KV_GROUPS * kk, NSA_DH), BF16),
            pltpu.VMEM((hh * qp, 1), F32),
            pltpu.VMEM((hh * qp, 1), F32),
            pltpu.VMEM((hh * qp, MLA_KV_RANK), F32),
            pltpu.VMEM((NSA_KV_GROUPS * rows, 1), F32),
            pltpu.VMEM((NSA_KV_GROUPS * rows, 1), F32),
            pltpu.VMEM((NSA_KV_GROUPS * rows, NSA_DH), F32),
        ],
    )
    return pl.pallas_call(
        functools.partial(_sample_attend_kernel, pps=pps, nch=nch, past=past, nq=nq, win_buf=win_buf),
        grid_spec=grid_spec,
        out_shape=[
            jax.ShapeDtypeStruct((db, qp, hh * MLA_V), BF16),
            jax.ShapeDtypeStruct((db, qp, NSA_HEADS * NSA_DH), BF16),
        ],
        compiler_params=_params("parallel", "arbitrary"),
        name="sample_attend",
    )(page_table.reshape(-1), *([ckv_pool] * pps), *([kr_pool] * pps), *([ks_pool] * pps), *([vs_pool] * pps),
      ql, qr, qn, selc, sell, oc, gt, ckvn, krn, ksn, vsn, kw, vw, expand, wuv)


def _outproj_kernel(x_ref, a_ref, b_ref, wa_ref, wb_ref, o_ref):
    o_ref[...] = x_ref[...] + _dot(a_ref[...], wa_ref[...]) + _dot(b_ref[...], wb_ref[...])


def _outproj(x, a, b, wa, wb):
    n, d = x.shape
    tm = _row_tile(n, 512)
    return pl.pallas_call(
        _outproj_kernel,
        grid=(n // tm,),
        in_specs=[
            pl.BlockSpec((tm, d), lambda i: (i, 0)),
            pl.BlockSpec((tm, a.shape[1]), lambda i: (i, 0)),
            pl.BlockSpec((tm, b.shape[1]), lambda i: (i, 0)),
            pl.BlockSpec(wa.shape, lambda i: (0, 0)),
            pl.BlockSpec(wb.shape, lambda i: (0, 0)),
        ],
        out_specs=pl.BlockSpec((tm, d), lambda i: (i, 0)),
        out_shape=jax.ShapeDtypeStruct((n, d), F32),
        compiler_params=_params("parallel"),
        name="outproj",
    )(x, a, b, wa, wb)


def _rope_tables(pos, rot_dim):
    inv = ROPE_THETA ** (-jnp.arange(0, rot_dim, 2, dtype=F32) / rot_dim)
    ang = pos.astype(F32)[:, None] * inv[None, :]
    return jnp.cos(ang), jnp.sin(ang)


def _rope(x, cos, sin):
    half = cos.shape[-1]
    c = cos[:, None, :]
    s = sin[:, None, :]
    x1, x2 = x[..., :half], x[..., half:2 * half]
    return jnp.concatenate([x1 * c - x2 * s, x2 * c + x1 * s, x[..., 2 * half:]], axis=-1)


def _pad_queries(a, db, nq):
    heads, d = a.shape[1:]
    a = a.reshape(db, nq, heads, d).transpose(0, 2, 1, 3)
    a = jnp.pad(a, ((0, 0), (0, 0), (0, QUERY_PAD - nq), (0, 0)))
    return a.reshape(db, heads * QUERY_PAD, d)


def _pad_new(a, db, nq):
    a = a.reshape(db, nq, a.shape[-1])
    return jnp.pad(a, ((0, 0), (0, NEW_PAD - nq), (0, 0))).astype(BF16)


def _pad_new_groups(a, db, nq):
    a = a.reshape(db, nq, NSA_KV_GROUPS, NSA_DH).transpose(0, 2, 1, 3)
    a = jnp.pad(a, ((0, 0), (0, 0), (0, NEW_PAD - nq), (0, 0)))
    return a.reshape(db, NSA_KV_GROUPS * NEW_PAD, NSA_DH).astype(BF16)


def kernel(x_prompt, x_sample, cache_mla_ckv, cache_mla_krope, cache_nsa_k_cmp, cache_nsa_v_cmp, cache_nsa_k_sel, cache_nsa_v_sel, state_nsa_k_win, state_nsa_v_win, page_table, ffn1_norm, w_ffn1_gate, w_ffn1_up, w_ffn1_down, mix_norm, w_in, mla_q_norm, w_mla_q_up, mla_kv_norm, w_mla_k_up, w_mla_v_up, nsa_cmp_pos_k, nsa_cmp_lin_k, nsa_cmp_pos_v, nsa_cmp_lin_v, w_out, ffn2_norm, w_ffn2_gate, w_ffn2_up, w_ffn2_down, final_norm):
    nb, seq, d = x_prompt.shape
    db, nq = x_sample.shape[:2]
    depth, n_pool = cache_mla_ckv.shape[:2]
    npages = page_table.shape[1]
    past = npages * PAGE_ROWS
    win_buf = state_nsa_k_win.shape[2]
    np_tok = nb * seq
    ns_tok = db * nq
    assert nq <= QUERY_PAD and cache_mla_ckv.shape[2] == PAGE_ROWS and seq % QUERY_BLOCK == 0
    hh, gg, dh = MLA_HEADS, NSA_KV_GROUPS, NSA_DH
    x = jnp.concatenate([x_prompt.reshape(np_tok, d), x_sample.reshape(ns_tok, d)], axis=0)
    pos = jnp.concatenate([jnp.tile(jnp.arange(seq), nb), jnp.tile(past + jnp.arange(nq), db)])
    cos_m, sin_m = _rope_tables(pos, MLA_ROPE)
    cos_n, sin_n = _rope_tables(pos, NSA_ROT)

    ns_s = -(-(past + nq) // SEL_BLOCK)
    selw = -(-ns_s // LANE) * LANE
    nseg_s = past // CMP_STRIDE
    c_st = np.arange(nseg_s)[:, None] * CMP_STRIDE
    s_st = np.arange(selw)[None, :] * SEL_BLOCK
    overlap = jnp.asarray(((c_st < s_st + SEL_BLOCK) & (c_st + CMP_BLOCK > s_st)), dtype=BF16)
    pps = min(PAGES_PER_STEP, npages)
    nch = npages // pps
    bpc = pps * PAGE_ROWS // SEL_BLOCK
    assert gg * bpc <= LANE
    e_row = np.arange(LANE)[:, None]
    e_col = np.arange(gg * pps * PAGE_ROWS)[None, :]
    expand = jnp.asarray((e_row // bpc == e_col % gg) & (e_row % bpc == e_col // (gg * SEL_BLOCK)), dtype=BF16)

    rows_p = [[] for _ in range(8)]
    rows_s = [[] for _ in range(8)]
    for l in range(depth):
        wq = w_mla_q_up[l].reshape(MLA_Q_RANK, hh, MLA_NOPE + MLA_ROPE)
        wqn = wq[:, :, :MLA_NOPE].reshape(MLA_Q_RANK, hh * MLA_NOPE).astype(BF16)
        wqr = wq[:, :, MLA_NOPE:].reshape(MLA_Q_RANK, hh * MLA_ROPE).astype(BF16)
        wuk = w_mla_k_up[l].reshape(MLA_KV_RANK, hh, MLA_NOPE).transpose(1, 2, 0).astype(BF16)
        wuv = w_mla_v_up[l].reshape(MLA_KV_RANK, hh, MLA_V).transpose(1, 0, 2).astype(BF16)
        wi = w_in[l]
        o_q, o_kv, o_kr = 0, MLA_Q_RANK, MLA_Q_RANK + MLA_KV_RANK
        o_qn = o_kr + MLA_ROPE
        o_kv6 = o_qn + NSA_HEADS * dh
        o_g = o_kv6 + 6 * KV_COLS
        w_in_p = jnp.concatenate([
            wi[:, o_q:o_kr], wi[:, o_qn:o_g], wi[:, o_kr:o_qn], wi[:, o_g:],
            jnp.zeros((d, IN_PAD - wi.shape[1]), wi.dtype)], axis=1).astype(BF16)
        pwk, pwv = (jnp.repeat(w, NSA_KV_GROUPS, axis=0) for w in (nsa_cmp_pos_k[l], nsa_cmp_pos_v[l]))
        link = nsa_cmp_lin_k[l].astype(BF16)
        linv = nsa_cmp_lin_v[l].astype(BF16)

        x = _ffn(x, ffn1_norm[l], w_ffn1_gate[l].astype(BF16), w_ffn1_up[l].astype(BF16),
                 w_ffn1_down[l].astype(BF16))
        z = _inproj(x, mix_norm[l], w_in_p)
        ql, qr_raw, ckv = _mlaprep(z, mla_q_norm[l], mla_kv_norm[l], wqn, wqr, wuk)

        qr = _rope(qr_raw.reshape(-1, hh, MLA_ROPE), cos_m, sin_m)
        krope = _rope(z[:, None, COL_KR:COL_KR + MLA_ROPE], cos_m, sin_m)[:, 0]
        qn = _rope(z[:, COL_QN:COL_KC].reshape(-1, NSA_HEADS, dh), cos_n, sin_n)

        def kv_rot(col):
            return _rope(z[:, col:col + KV_COLS].reshape(-1, gg, dh), cos_n, sin_n).reshape(-1, KV_COLS)

        k_cmp, k_sel, k_win = kv_rot(COL_KC), kv_rot(COL_KS), kv_rot(COL_KW)
        v_cmp, v_sel, v_win = (z[:, c0:c0 + KV_COLS] for c0 in (COL_VC, COL_VS, COL_VW))
        ckv_b, krope_b = ckv.astype(BF16), krope.astype(BF16)
        k_sel_b, v_sel_b, k_win_b, v_win_b = (a.astype(BF16) for a in (k_sel, v_sel, k_win, v_win))

        qr_t = qr.transpose(1, 0, 2).astype(BF16)
        qn_t = qn.transpose(1, 0, 2).astype(BF16)
        o_mla_p = _mla_prompt(ql, qr_t, ckv_b, krope_b, wuv, nb, seq)
        kcmp_p, vcmp_p = _compress_prompt(k_cmp, v_cmp, pwk, pwv, link, linv, nb, seq)
        o_nsa_p = _nsa_prompt(qn_t, kcmp_p, vcmp_p, k_sel_b, v_sel_b, k_win_b, v_win_b, z, nb, seq)

        sm = slice(np_tok, None)
        ql_s = _pad_queries(ql[:, sm].transpose(1, 0, 2), db, nq)
        qr_s = _pad_queries(qr[sm], db, nq).astype(BF16)
        qn_s = _pad_queries(qn[sm], db, nq).astype(BF16)
        gates = jax.nn.sigmoid(z[sm, COL_G:COL_G + GATE_COLS]).reshape(ns_tok, NSA_HEADS, 3)
        gt_s = _pad_queries(gates, db, nq)
        ckv_pool = cache_mla_ckv[l]
        kr_pool = jnp.swapaxes(cache_mla_krope[l], 1, 2)
        kc_pool, vc_pool, ks_pool, vs_pool = (c[l].reshape(n_pool, PAGE_IROWS, dh) for c in (
            cache_nsa_k_cmp, cache_nsa_v_cmp, cache_nsa_k_sel, cache_nsa_v_sel))
        o_c, sel = _sample_select(page_table, kc_pool, vc_pool, qn_s.reshape(db, gg, NSA_HPG * QUERY_PAD, dh),
                                  pwk, pwv, link, linv, overlap, past, nq, ns_s, min(SEL_TOPK, ns_s))
        selc = sel[:, :, :nch * bpc].reshape(db, gg, QUERY_PAD, nch, bpc).transpose(0, 3, 1, 2, 4)
        selc = selc[:, :, :, :, None, :] * jnp.eye(gg, dtype=F32)[None, None, :, None, :, None]
        selc = selc.reshape(db, nch, gg * QUERY_PAD, gg * bpc)
        selc = jnp.pad(selc, ((0, 0), (0, 0), (0, 0), (0, LANE - gg * bpc)))
        sell = jnp.broadcast_to(sel[:, :, nch * bpc:nch * bpc + 1], (db, gg * QUERY_PAD, LANE))
        kwin_full = jnp.concatenate([state_nsa_k_win[l].reshape(db, win_buf, KV_COLS),
                                     k_win[sm].reshape(db, nq, KV_COLS)], axis=1)
        vwin_full = jnp.concatenate([state_nsa_v_win[l].reshape(db, win_buf, KV_COLS),
                                     v_win[sm].reshape(db, nq, KV_COLS)], axis=1)
        wpad = ((0, 0), (0, NEW_PAD - nq), (0, 0))
        o_mla_s, o_nsa_s = _sample_attend(
            page_table, ckv_pool, kr_pool, ks_pool, vs_pool, ql_s, qr_s, qn_s, selc, sell,
            o_c.reshape(db, -1, dh), gt_s,
            _pad_new(ckv[sm], db, nq), _pad_new(krope[sm], db, nq), _pad_new_groups(k_sel[sm], db, nq),
            _pad_new_groups(v_sel[sm], db, nq), jnp.pad(kwin_full, wpad).astype(BF16),
            jnp.pad(vwin_full, wpad).astype(BF16), expand, wuv, past, nq, win_buf)

        mix_a = jnp.concatenate([o_mla_p, o_mla_s[:, :nq].reshape(ns_tok, -1)], axis=0)
        mix_b = jnp.concatenate([o_nsa_p, o_nsa_s[:, :nq].reshape(ns_tok, -1)], axis=0)
        wo = w_out[l].astype(BF16)
        x = _outproj(x, mix_a, mix_b, wo[:hh * MLA_V], wo[hh * MLA_V:])
        x = _ffn(x, ffn2_norm[l], w_ffn2_gate[l].astype(BF16), w_ffn2_up[l].astype(BF16),
                 w_ffn2_down[l].astype(BF16), final_g=final_norm if l == depth - 1 else None)

        win_p = min(WINDOW, seq)
        new = (ckv, krope, k_cmp, v_cmp, k_sel, v_sel)
        for i, a in enumerate(new):
            tail = a.shape[1:] if i < 2 else (gg, dh)
            rows_p[i].append(a[:np_tok].reshape((nb, seq) + tail))
            rows_s[i].append(a[np_tok:].reshape((db, nq) + tail))
        rows_p[6].append(k_win[:np_tok].reshape(nb, seq, gg, dh)[:, -win_p:])
        rows_p[7].append(v_win[:np_tok].reshape(nb, seq, gg, dh)[:, -win_p:])
        rows_s[6].append(kwin_full[:, -win_buf:].reshape(db, win_buf, gg, dh))
        rows_s[7].append(vwin_full[:, -win_buf:].reshape(db, win_buf, gg, dh))

    y_prompt = x[:np_tok].reshape(nb, seq, d)
    y_sample = x[np_tok:].reshape(db, nq, d)
    out = [y_prompt, y_sample]
    for i in range(8):
        out += [jnp.stack(rows_p[i]), jnp.stack(rows_s[i])]
    return tuple(out)
```

```python
import functools

import jax
import jax.numpy as jnp
import numpy as np
from jax import lax
from jax.experimental import pallas as pl
from jax.experimental.pallas import tpu as pltpu

MLA_HEADS = 8
MLA_NOPE = 128
MLA_ROPE = 64
MLA_V = 128
MLA_Q_RANK = 512
MLA_KV_RANK = 256
NSA_HEADS = 8
NSA_KV_GROUPS = 2
NSA_HPG = NSA_HEADS // NSA_KV_GROUPS
NSA_DH = 128
CMP_BLOCK = 32
CMP_STRIDE = 16
SEL_BLOCK = 64
SEL_TOPK = 16
SEL_LOCAL = 2
WINDOW = 512
ROPE_THETA = 500000.0
NSA_ROT = NSA_DH // 4
RMS_EPS = 1e-6
QUERY_BLOCK = 128
MLA_SCALE = (MLA_NOPE + MLA_ROPE) ** -0.5
NSA_SCALE = NSA_DH ** -0.5
LOG2E = 1.4426950408889634
MLA_QSCALE = MLA_SCALE * LOG2E
NSA_QSCALE = NSA_SCALE * LOG2E
NEG_INF = -1e30
BIG = 1e6
TINY = 1e-30
KV_COLS = NSA_KV_GROUPS * NSA_DH
GATE_COLS = 3 * NSA_HEADS

COL_Q = 0
COL_KV = COL_Q + MLA_Q_RANK
COL_QN = COL_KV + MLA_KV_RANK
COL_KC = COL_QN + NSA_HEADS * NSA_DH
COL_VC = COL_KC + KV_COLS
COL_KS = COL_VC + KV_COLS
COL_VS = COL_KS + KV_COLS
COL_KW = COL_VS + KV_COLS
COL_VW = COL_KW + KV_COLS
COL_KR = COL_VW + KV_COLS
COL_G = COL_KR + MLA_ROPE
LANE = 128
IN_PAD = -(-(COL_G + GATE_COLS) // (3 * LANE)) * (3 * LANE)

QUERY_PAD = 8
NEW_PAD = 16
PAGE_ROWS = 128
SEL_SHIFT = SEL_BLOCK.bit_length() - 1
NEW_SHIFT = NEW_PAD.bit_length() - 1
PAGES_PER_STEP = 16
MLA_KEY_CHUNK = 1024
MLA_HEAD_GROUP = 4
NSA_KEY_CHUNK = 1024
NSA_HEAD_GROUP = 4
VMEM_LIMIT = 56 * 1024 * 1024

F32 = jnp.float32
BF16 = jnp.bfloat16


def _dot(a, b):
    return jnp.dot(a, b, preferred_element_type=F32)


def _dot_nt(a, b):
    return lax.dot_general(a, b, (((1,), (1,)), ((), ())), preferred_element_type=F32)


def _rms(x, g):
    return x * lax.rsqrt(jnp.mean(x * x, axis=-1, keepdims=True) + RMS_EPS) * g


def _split3_dot_nt(w_bf16, x):
    hi = x.astype(BF16)
    r1 = x - hi.astype(F32)
    mid = r1.astype(BF16)
    lo = (r1 - mid.astype(F32)).astype(BF16)
    return _dot_nt(w_bf16, hi) + _dot_nt(w_bf16, mid) + _dot_nt(w_bf16, lo)


def _split3_dot(x, w_bf16):
    hi = x.astype(BF16)
    r1 = x - hi.astype(F32)
    mid = r1.astype(BF16)
    lo = (r1 - mid.astype(F32)).astype(BF16)
    return _dot(hi, w_bf16) + _dot(mid, w_bf16) + _dot(lo, w_bf16)


def _row_tile(n, want):
    t = min(want, n)
    while n % t:
        t -= 8
    return t


def _params(*sem):
    return pltpu.CompilerParams(dimension_semantics=sem, vmem_limit_bytes=VMEM_LIMIT)


def _ffn_kernel(x_ref, g_ref, wg_ref, wu_ref, wd_ref, *rest, final):
    if final:
        fg_ref, o_ref, h_ref, acc_ref = rest
    else:
        o_ref, h_ref, acc_ref = rest
    j = pl.program_id(1)

    @pl.when(j == 0)
    def _():
        h_ref[...] = _rms(x_ref[...], g_ref[...]).astype(BF16)
        acc_ref[...] = jnp.zeros_like(acc_ref)

    h = h_ref[...]
    a = _dot(h, wg_ref[...])
    u = _dot(h, wu_ref[...])
    act = (a / (1.0 + jnp.exp(-a))) * u
    acc_ref[...] += _dot(act.astype(BF16), wd_ref[...])

    @pl.when(j == pl.num_programs(1) - 1)
    def _():
        y = x_ref[...] + 0.5 * acc_ref[...]
        if final:
            y = _rms(y, fg_ref[...])
        o_ref[...] = y


def _ffn(x, g, wg, wu, wd, final_g=None):
    n, d = x.shape
    dff = wg.shape[1]
    tm = _row_tile(n, 512)
    tf = 512 if dff % 512 == 0 else dff
    in_specs = [
        pl.BlockSpec((tm, d), lambda i, j: (i, 0)),
        pl.BlockSpec((1, d), lambda i, j: (0, 0)),
        pl.BlockSpec((d, tf), lambda i, j: (0, j)),
        pl.BlockSpec((d, tf), lambda i, j: (0, j)),
        pl.BlockSpec((tf, d), lambda i, j: (j, 0)),
    ]
    args = [x, g.reshape(1, d), wg, wu, wd]
    if final_g is not None:
        in_specs.append(pl.BlockSpec((1, d), lambda i, j: (0, 0)))
        args.append(final_g.reshape(1, d))
    return pl.pallas_call(
        functools.partial(_ffn_kernel, final=final_g is not None),
        grid=(n // tm, dff // tf),
        in_specs=in_specs,
        out_specs=pl.BlockSpec((tm, d), lambda i, j: (i, 0)),
        out_shape=jax.ShapeDtypeStruct((n, d), F32),
        scratch_shapes=[pltpu.VMEM((tm, d), BF16), pltpu.VMEM((tm, d), F32)],
        compiler_params=_params("parallel", "arbitrary"),
        name="ffn",
    )(*args)


def _inproj_kernel(x_ref, g_ref, w_ref, o_ref, h_ref):
    @pl.when(pl.program_id(1) == 0)
    def _():
        h_ref[...] = _rms(x_ref[...], g_ref[...]).astype(BF16)

    o_ref[...] = _dot(h_ref[...], w_ref[...])


def _inproj(x, g, w):
    n, d = x.shape
    nout = w.shape[1]
    tm = _row_tile(n, 512)
    tn = nout // 3
    return pl.pallas_call(
        _inproj_kernel,
        grid=(n // tm, nout // tn),
        in_specs=[
            pl.BlockSpec((tm, d), lambda i, j: (i, 0)),
            pl.BlockSpec((1, d), lambda i, j: (0, 0)),
            pl.BlockSpec((d, tn), lambda i, j: (0, j)),
        ],
        out_specs=pl.BlockSpec((tm, tn), lambda i, j: (i, j)),
        out_shape=jax.ShapeDtypeStruct((n, nout), F32),
        scratch_shapes=[pltpu.VMEM((tm, d), BF16)],
        compiler_params=_params("parallel", "arbitrary"),
        name="inproj",
    )(x, g.reshape(1, d), w)


def _mlaprep_kernel(zq_ref, zkv_ref, gq_ref, gkv_ref, wqn_ref, wqr_ref, wuk_ref, ql_ref, qr_ref, ckv_ref):
    cq = _rms(zq_ref[...], gq_ref[...]).astype(BF16)
    qn = _dot(cq, wqn_ref[...]).astype(BF16)
    for h in range(MLA_HEADS):
        ql_ref[h] = (_dot(qn[:, h * MLA_NOPE:(h + 1) * MLA_NOPE], wuk_ref[h]) * MLA_QSCALE).astype(BF16)
    qr_ref[...] = _dot(cq, wqr_ref[...])
    ckv_ref[...] = _rms(zkv_ref[...], gkv_ref[...])


def _mlaprep(z, gq, gkv, wqn, wqr, wuk):
    n = z.shape[0]
    tm = _row_tile(n, 512)
    return pl.pallas_call(
        _mlaprep_kernel,
        grid=(n // tm,),
        in_specs=[
            pl.BlockSpec((tm, MLA_Q_RANK), lambda i: (i, COL_Q // MLA_Q_RANK)),
            pl.BlockSpec((tm, MLA_KV_RANK), lambda i: (i, COL_KV // MLA_KV_RANK)),
            pl.BlockSpec((1, MLA_Q_RANK), lambda i: (0, 0)),
            pl.BlockSpec((1, MLA_KV_RANK), lambda i: (0, 0)),
            pl.BlockSpec(wqn.shape, lambda i: (0, 0)),
            pl.BlockSpec(wqr.shape, lambda i: (0, 0)),
            pl.BlockSpec(wuk.shape, lambda i: (0, 0, 0)),
        ],
        out_specs=[
            pl.BlockSpec((MLA_HEADS, tm, MLA_KV_RANK), lambda i: (0, i, 0)),
            pl.BlockSpec((tm, MLA_HEADS * MLA_ROPE), lambda i: (i, 0)),
            pl.BlockSpec((tm, MLA_KV_RANK), lambda i: (i, 0)),
        ],
        out_shape=[
            jax.ShapeDtypeStruct((MLA_HEADS, n, MLA_KV_RANK), BF16),
            jax.ShapeDtypeStruct((n, MLA_HEADS * MLA_ROPE), F32),
            jax.ShapeDtypeStruct((n, MLA_KV_RANK), F32),
        ],
        compiler_params=_params("parallel"),
        name="mlaprep",
    )(z, z, gq.reshape(1, -1), gkv.reshape(1, -1), wqn, wqr, wuk)


def _mla_prompt_kernel(ql_ref, qr_ref, ckv_ref, kr_ref, wuv_ref, o_ref, m_ref, l_ref, acc_ref, *, tq, kc):
    q0 = pl.program_id(1) * tq
    hh = MLA_HEADS
    m_ref[...] = jnp.full(m_ref.shape, NEG_INF, F32)
    l_ref[...] = jnp.zeros_like(l_ref)
    acc_ref[...] = jnp.zeros_like(acc_ref)
    t = q0 + lax.broadcasted_iota(jnp.int32, (tq, kc), 0)
    koff = lax.broadcasted_iota(jnp.int32, (tq, kc), 1)

    def chunk(c, masked):
        k0 = pl.multiple_of(c * kc, kc)
        kk = ckv_ref[pl.ds(k0, kc), :]
        kr = kr_ref[pl.ds(k0, kc), :]
        hg = MLA_HEAD_GROUP
        for h0 in range(0, hh, hg):
            hs = slice(h0, h0 + hg)
            ql = ql_ref[hs].reshape(hg * tq, MLA_KV_RANK)
            qr = qr_ref[hs].reshape(hg * tq, MLA_ROPE)
            s = (_dot_nt(ql, kk) + _dot_nt(qr, kr)).reshape(hg, tq, kc)
            if masked:
                s = jnp.where(koff + k0 <= t, s, NEG_INF)
            m_prev = m_ref[hs]
            m_new = jnp.maximum(m_prev, jnp.max(s, axis=-1, keepdims=True))
            alpha = jnp.exp2(m_prev - m_new)
            p = jnp.exp2(s - m_new)
            l_ref[hs] = alpha * l_ref[hs] + jnp.sum(p, axis=-1, keepdims=True)
            pv = _dot(p.reshape(hg * tq, kc).astype(BF16), kk)
            acc_ref[hs] = alpha * acc_ref[hs] + pv.reshape(hg, tq, MLA_KV_RANK)
            m_ref[hs] = m_new

    def body(c, carry):
        chunk(c, False)
        return carry

    lax.fori_loop(0, q0 // kc, body, 0)
    chunk(q0 // kc, True)
    for h in range(hh):
        o = (acc_ref[h] / l_ref[h]).astype(BF16)
        o_ref[:, h * MLA_V:(h + 1) * MLA_V] = _dot(o, wuv_ref[h]).astype(o_ref.dtype)


def _mla_prompt(ql, qr, ckv, kr, wuv, nb, seq):
    tq = min(QUERY_BLOCK, seq)
    kc = min(MLA_KEY_CHUNK, seq)
    assert kc % tq == 0
    nq = seq // tq
    hh = MLA_HEADS
    return pl.pallas_call(
        functools.partial(_mla_prompt_kernel, tq=tq, kc=kc),
        grid=(nb, nq),
        in_specs=[
            pl.BlockSpec((hh, tq, MLA_KV_RANK), lambda b, i: (0, b * nq + i, 0)),
            pl.BlockSpec((hh, tq, MLA_ROPE), lambda b, i: (0, b * nq + i, 0)),
            pl.BlockSpec((seq, MLA_KV_RANK), lambda b, i: (b, 0)),
            pl.BlockSpec((seq, MLA_ROPE), lambda b, i: (b, 0)),
            pl.BlockSpec(wuv.shape, lambda b, i: (0, 0, 0)),
        ],
        out_specs=pl.BlockSpec((tq, hh * MLA_V), lambda b, i: (b * nq + i, 0)),
        out_shape=jax.ShapeDtypeStruct((nb * seq, hh * MLA_V), BF16),
        scratch_shapes=[
            pltpu.VMEM((hh, tq, 1), F32),
            pltpu.VMEM((hh, tq, 1), F32),
            pltpu.VMEM((hh, tq, MLA_KV_RANK), F32),
        ],
        compiler_params=_params("parallel", "arbitrary"),
        name="mla_prompt",
    )(ql, qr, ckv, kr, wuv)


SUBLANES = 8
TILES_PER_SEG = NSA_KV_GROUPS * CMP_STRIDE // SUBLANES
SEGS_PER_PAGE = PAGE_ROWS // CMP_STRIDE
PAGE_IROWS = NSA_KV_GROUPS * PAGE_ROWS


def _page_segment_sums(x_ref, row0, wt_ref):
    sub = lax.broadcasted_iota(jnp.int32, (SUBLANES, NSA_DH), 0)
    out = [[jnp.zeros((SEGS_PER_PAGE, NSA_DH), F32) for _ in range(NSA_KV_GROUPS)] for _ in range(2)]
    for n in range(SEGS_PER_PAGE):
        tiles = [x_ref[pl.ds(row0 + (n * TILES_PER_SEG + k) * SUBLANES, SUBLANES), :] for k in range(TILES_PER_SEG)]
        for m in range(2):
            w0 = m * TILES_PER_SEG * SUBLANES
            p = tiles[0] * wt_ref[w0:w0 + SUBLANES, :]
            for k in range(1, TILES_PER_SEG):
                p = p + tiles[k] * wt_ref[w0 + k * SUBLANES:w0 + (k + 1) * SUBLANES, :]
            p = p + pltpu.roll(p, 4, 0)
            p = p + pltpu.roll(p, 2, 0)
            q = pltpu.roll(p, 1, 0)
            for g in range(NSA_KV_GROUPS):
                out[m][g] = jnp.where(sub == n, p if n % 2 == g else q, out[m][g])
    return out


def _compress_prompt_kernel(k_ref, v_ref, wtk_ref, wtv_ref, link_ref, linv_ref, kc_ref, vc_ref, *, nseg):
    npg = nseg // SEGS_PER_PAGE
    for x_ref, wt_ref, lin_ref, o_ref in ((k_ref, wtk_ref, link_ref, kc_ref), (v_ref, wtv_ref, linv_ref, vc_ref)):
        parts = [_page_segment_sums(x_ref, pg * PAGE_IROWS, wt_ref) for pg in range(npg)]
        for g in range(NSA_KV_GROUPS):
            a0 = jnp.concatenate([pt[0][g] for pt in parts], axis=0)
            a1 = jnp.concatenate([pt[1][g] for pt in parts], axis=0)
            acc = (a0 + pltpu.roll(a1, nseg - 1, 0)).astype(BF16)
            o_ref[g] = _dot(acc, lin_ref[...]).astype(BF16)


def _compress_prompt(k, v, pwk, pwv, link, linv, nb, seq):
    nseg = seq // CMP_STRIDE
    row = pl.BlockSpec((NSA_KV_GROUPS * seq, NSA_DH), lambda b: (b, 0))
    pw = pl.BlockSpec((NSA_KV_GROUPS * CMP_BLOCK, NSA_DH), lambda b: (0, 0))
    lin = pl.BlockSpec((NSA_DH, NSA_DH), lambda b: (0, 0))
    out = pl.BlockSpec((None, NSA_KV_GROUPS, nseg, NSA_DH), lambda b: (b, 0, 0, 0))
    shp = jax.ShapeDtypeStruct((nb, NSA_KV_GROUPS, nseg, NSA_DH), BF16)
    return pl.pallas_call(
        functools.partial(_compress_prompt_kernel, nseg=nseg),
        grid=(nb,),
        in_specs=[row, row, pw, pw, lin, lin],
        out_specs=[out, out],
        out_shape=[shp, shp],
        compiler_params=_params("parallel"),
        name="compress_prompt",
    )(k.reshape(-1, NSA_DH), v.reshape(-1, NSA_DH), pwk, pwv, link, linv)


def _masked_softmax(s, mask):
    s = jnp.where(mask, s, NEG_INF)
    e = jnp.where(mask, jnp.exp2(s - jnp.max(s, axis=-1, keepdims=True)), 0.0)
    return e / jnp.maximum(jnp.sum(e, axis=-1, keepdims=True), TINY)


def _forced_importance(imp, blk, cur, ns):
    valid = blk <= cur
    forced = valid & ((blk == 0) | (blk > cur - SEL_LOCAL))
    imp = jnp.where(forced, BIG, jnp.where(valid, imp, -BIG))
    return jnp.where(blk < ns, imp, -2.0 * BIG)


def _nsa_prompt_kernel(qn_ref, kcmp_ref, vcmp_ref, ks_ref, vs_ref, kw_ref, vw_ref, zg_ref, o_ref,
                       m_ref, l_ref, acc_ref, *, tq, seq, kc, ntop):
    g = pl.program_id(1)
    q0 = pl.program_id(2) * tq
    jj = NSA_HPG
    nseg = seq // CMP_STRIDE
    ns = seq // SEL_BLOCK

    q = qn_ref[...].reshape(jj * tq, NSA_DH)
    t2 = q0 + lax.broadcasted_iota(jnp.int32, (tq, nseg), 0)
    cmp_last = lax.broadcasted_iota(jnp.int32, (tq, nseg), 1) * CMP_STRIDE + (CMP_BLOCK - 1)
    pc = _masked_softmax(_dot_nt(q, kcmp_ref[...]).reshape(jj, tq, nseg), cmp_last <= t2)
    o_c = _dot(pc.reshape(jj * tq, nseg).astype(BF16), vcmp_ref[...]).reshape(jj, tq, NSA_DH)
    psum = jnp.sum(pc, axis=0)

    blk_r = lax.broadcasted_iota(jnp.int32, (ns, nseg), 0) * SEL_BLOCK
    c_st = lax.broadcasted_iota(jnp.int32, (ns, nseg), 1) * CMP_STRIDE
    ov_t = jnp.where((c_st < blk_r + SEL_BLOCK) & (c_st + CMP_BLOCK > blk_r), 1.0, 0.0).astype(BF16)
    imp_t = _split3_dot_nt(ov_t, psum)
    blk = lax.broadcasted_iota(jnp.int32, (ns, tq), 0)
    cur = (q0 + lax.broadcasted_iota(jnp.int32, (ns, tq), 1)) >> SEL_SHIFT
    imp_t = _forced_importance(imp_t, blk, cur, ns)
    rank = jnp.zeros((ns, tq), F32)
    for mm in range(ns):
        row = imp_t[mm:mm + 1, :]
        tie = jnp.where(blk > mm, 1.0, 0.0)
        rank = rank + jnp.where(row > imp_t, 1.0, jnp.where(row == imp_t, tie, 0.0))
    sel_t = jnp.where(rank < ntop, 1.0, 0.0)
    sel = jnp.concatenate([sel_t, jnp.zeros((LANE - ns, tq), F32)], axis=0).T.astype(BF16)

    m_ref[...] = jnp.full(m_ref.shape, NEG_INF, F32)
    l_ref[...] = jnp.zeros_like(l_ref)
    acc_ref[...] = jnp.zeros_like(acc_ref)
    tk = q0 + lax.broadcasted_iota(jnp.int32, (tq, kc), 0)
    koff = lax.broadcasted_iota(jnp.int32, (tq, kc), 1)
    e_row = lax.broadcasted_iota(jnp.int32, (LANE, kc), 0)
    e_col = lax.broadcasted_iota(jnp.int32, (LANE, kc), 1) >> SEL_SHIFT

    def body(c, carry):
        k0 = pl.multiple_of(c * kc, kc)
        kk = ks_ref[pl.ds(k0, kc), :]
        vv = vs_ref[pl.ds(k0, kc), :]
        expand = jnp.where(e_row == e_col + c * (kc // SEL_BLOCK), 1.0, 0.0).astype(BF16)
        visible = (_dot(sel, expand) > 0.5) & (koff + k0 <= tk)
        hg = NSA_HEAD_GROUP
        for j0 in range(0, jj, hg):
            js = slice(j0, j0 + hg)
            s = _dot_nt(qn_ref[js].reshape(hg * tq, NSA_DH), kk).reshape(hg, tq, kc)
            s = jnp.where(visible, s, NEG_INF)
            m_prev = m_ref[js]
            m_new = jnp.maximum(m_prev, jnp.max(s, axis=-1, keepdims=True))
            alpha = jnp.exp2(m_prev - m_new)
            p = jnp.exp2(s - m_new)
            l_ref[js] = alpha * l_ref[js] + jnp.sum(p, axis=-1, keepdims=True)
            pv = _dot(p.reshape(hg * tq, kc).astype(BF16), vv)
            acc_ref[js] = alpha * acc_ref[js] + pv.reshape(hg, tq, NSA_DH)
            m_ref[js] = m_new
        return carry

    lax.fori_loop(0, (q0 + tq + kc - 1) // kc, body, 0)

    wl = min(WINDOW + tq, seq)
    w0 = pl.multiple_of(jnp.clip(q0 - WINDOW, 0, seq - wl), tq)
    kw = kw_ref[pl.ds(w0, wl), :]
    vw = vw_ref[pl.ds(w0, wl), :]
    tw = q0 + lax.broadcasted_iota(jnp.int32, (tq, wl), 0)
    wpos = w0 + lax.broadcasted_iota(jnp.int32, (tq, wl), 1)
    win_ok = (wpos >= tw - WINDOW) & (wpos <= tw)

    sw = jnp.where(win_ok, _dot_nt(q, kw).reshape(jj, tq, wl), NEG_INF)
    pw = jnp.exp2(sw - jnp.max(sw, axis=-1, keepdims=True))
    o_w = (_dot(pw.reshape(jj * tq, wl).astype(BF16), vw).reshape(jj, tq, NSA_DH)
           / jnp.sum(pw, axis=-1, keepdims=True))
    o_s = acc_ref[...] / l_ref[...]

    zg = zg_ref[...]
    lane = lax.broadcasted_iota(jnp.int32, zg.shape, 1)
    for j in range(jj):
        def gate(k, j=j):
            col = jnp.sum(jnp.where(lane == COL_G % LANE + (g * jj + j) * 3 + k, zg, 0.0), axis=-1, keepdims=True)
            return 1.0 / (1.0 + jnp.exp(-col))
        o = gate(0) * o_c[j] + gate(1) * o_s[j] + gate(2) * o_w[j]
        o_ref[:, j * NSA_DH:(j + 1) * NSA_DH] = o.astype(o_ref.dtype)


def _nsa_prompt(qn, kcmp, vcmp, ks, vs, kw, vw, z, nb, seq):
    tq = min(QUERY_BLOCK, seq)
    kc = min(NSA_KEY_CHUNK, seq)
    nq = seq // tq
    jj = NSA_HPG
    nseg = seq // CMP_STRIDE
    ntop = min(SEL_TOPK, seq // SEL_BLOCK)
    rows = pl.BlockSpec((seq, NSA_DH), lambda b, g, i: (b, g))
    cmp = pl.BlockSpec((None, None, nseg, NSA_DH), lambda b, g, i: (b, g, 0, 0))
    return pl.pallas_call(
        functools.partial(_nsa_prompt_kernel, tq=tq, seq=seq, kc=kc, ntop=ntop),
        grid=(nb, NSA_KV_GROUPS, nq),
        in_specs=[
            pl.BlockSpec((jj, tq, NSA_DH), lambda b, g, i: (g, b * nq + i, 0)),
            cmp, cmp, rows, rows, rows, rows,
            pl.BlockSpec((tq, LANE), lambda b, g, i: (b * nq + i, COL_G // LANE)),
        ],
        out_specs=pl.BlockSpec((tq, jj * NSA_DH), lambda b, g, i: (b * nq + i, g)),
        out_shape=jax.ShapeDtypeStruct((nb * seq, NSA_HEADS * NSA_DH), BF16),
        scratch_shapes=[
            pltpu.VMEM((jj, tq, 1), F32),
            pltpu.VMEM((jj, tq, 1), F32),
            pltpu.VMEM((jj, tq, NSA_DH), F32),
        ],
        compiler_params=_params("parallel", "parallel", "arbitrary"),
        name="nsa_prompt",
    )(qn, kcmp, vcmp, ks, vs, kw, vw, z)


def _sample_select_kernel(pt_ref, *refs, pps, nch, past, nq, ns, ntop):
    del pt_ref
    kp = refs[:pps]
    vp = refs[pps:2 * pps]
    (qn_ref, pwk_ref, pwv_ref, link_ref, linv_ref, ov_ref, oc_ref, sel_ref,
     a0k_ref, a1k_ref, a0v_ref, a1v_ref) = refs[2 * pps:]
    c = pl.program_id(1)
    segs = SEGS_PER_PAGE
    nseg = nch * pps * segs
    for p in range(pps):
        r0 = pl.multiple_of(c * (pps * segs) + p * segs, segs)
        for pg, pw_ref, a0_ref, a1_ref in ((kp[p], pwk_ref, a0k_ref, a1k_ref), (vp[p], pwv_ref, a0v_ref, a1v_ref)):
            part = _page_segment_sums(pg, 0, pw_ref)
            for g in range(NSA_KV_GROUPS):
                a0_ref[g, pl.ds(r0, segs), :] = part[0][g]
                a1_ref[g, pl.ds(r0, segs), :] = part[1][g]

    @pl.when(c == nch - 1)
    def _():
        qp = QUERY_PAD
        rows = NSA_HPG * qp
        qi = lax.broadcasted_iota(jnp.int32, (1, qp, nseg), 1)
        t3 = past + jnp.minimum(qi, nq - 1)
        cmp_last = lax.broadcasted_iota(jnp.int32, (1, qp, nseg), 2) * CMP_STRIDE + (CMP_BLOCK - 1)
        imps = []
        for g in range(NSA_KV_GROUPS):
            acc_k = (a0k_ref[g] + pltpu.roll(a1k_ref[g], nseg - 1, 0)).astype(BF16)
            acc_v = (a0v_ref[g] + pltpu.roll(a1v_ref[g], nseg - 1, 0)).astype(BF16)
            kc = _dot(acc_k, link_ref[...]).astype(BF16)
            vc = _dot(acc_v, linv_ref[...]).astype(BF16)
            sc = _dot_nt(qn_ref[g], kc).reshape(NSA_HPG, qp, nseg)
            pc = _masked_softmax(sc, cmp_last <= t3)
            oc_ref[g] = _dot(pc.reshape(rows, nseg).astype(BF16), vc)
            imps.append(_split3_dot(jnp.sum(pc, axis=0), ov_ref[...]))
        imp = jnp.concatenate(imps, axis=0)
        selw = imp.shape[1]
        blk = lax.broadcasted_iota(jnp.int32, imp.shape, 1)
        qrow = lax.broadcasted_iota(jnp.int32, imp.shape, 0) & (qp - 1)
        cur = (past + jnp.minimum(qrow, nq - 1)) >> SEL_SHIFT
        work = _forced_importance(imp, blk, cur, ns)
        blk_f = blk.astype(F32)
        chosen = jnp.zeros(imp.shape, F32)
        for _ in range(ntop):
            top = jnp.max(work, axis=-1, keepdims=True)
            first = jnp.min(jnp.where(work == top, blk_f, float(selw)), axis=-1, keepdims=True)
            hit = blk_f == first
            chosen = jnp.where(hit, 1.0, chosen)
            work = jnp.where(hit, -4.0 * BIG, work)
        sel_ref[...] = chosen


def _sample_select(page_table, kpool, vpool, qn, pwk, pwv, link, linv, ov, past, nq, ns, ntop):
    db, npages = page_table.shape
    pps = min(PAGES_PER_STEP, npages)
    nch = npages // pps
    nseg = npages * (PAGE_ROWS // CMP_STRIDE)
    rows = NSA_HPG * QUERY_PAD
    selw = ov.shape[1]
    pages = [pl.BlockSpec((None, NSA_KV_GROUPS * PAGE_ROWS, NSA_DH),
                          lambda b, c, pt, p=p: (pt[b * npages + c * pps + p], 0, 0)) for p in range(pps)]
    const2 = lambda b, c, pt: (0, 0)
    grid_spec = pltpu.PrefetchScalarGridSpec(
        num_scalar_prefetch=1,
        grid=(db, nch),
        in_specs=pages + pages + [
            pl.BlockSpec((None, NSA_KV_GROUPS, rows, NSA_DH), lambda b, c, pt: (b, 0, 0, 0)),
            pl.BlockSpec((NSA_KV_GROUPS * CMP_BLOCK, NSA_DH), const2),
            pl.BlockSpec((NSA_KV_GROUPS * CMP_BLOCK, NSA_DH), const2),
            pl.BlockSpec((NSA_DH, NSA_DH), const2),
            pl.BlockSpec((NSA_DH, NSA_DH), const2),
            pl.BlockSpec(ov.shape, const2),
        ],
        out_specs=[
            pl.BlockSpec((None, NSA_KV_GROUPS, rows, NSA_DH), lambda b, c, pt: (b, 0, 0, 0)),
            pl.BlockSpec((None, NSA_KV_GROUPS * QUERY_PAD, selw), lambda b, c, pt: (b, 0, 0)),
        ],
        scratch_shapes=[pltpu.VMEM((NSA_KV_GROUPS, nseg, NSA_DH), F32) for _ in range(4)],
    )
    return pl.pallas_call(
        functools.partial(_sample_select_kernel, pps=pps, nch=nch, past=past, nq=nq, ns=ns, ntop=ntop),
        grid_spec=grid_spec,
        out_shape=[
            jax.ShapeDtypeStruct((db, NSA_KV_GROUPS, rows, NSA_DH), F32),
            jax.ShapeDtypeStruct((db, NSA_KV_GROUPS * QUERY_PAD, selw), F32),
        ],
        compiler_params=_params("parallel", "arbitrary"),
        name="sample_select",
    )(page_table.reshape(-1), *([kpool] * pps), *([vpool] * pps), qn, pwk, pwv, link, linv, ov)


def _sample_attend_kernel(pt_ref, *refs, pps, nch, past, nq, win_buf):
    del pt_ref
    ckv_p = refs[:pps]
    kr_p = refs[pps:2 * pps]
    ks_p = refs[2 * pps:3 * pps]
    vs_p = refs[3 * pps:4 * pps]
    (ql_ref, qr_ref, qn_ref, selc_ref, sell_ref, oc_ref, gt_ref, ckvn_ref, krn_ref, ksn_ref, vsn_ref,
     kw_ref, vw_ref, e_ref, wuv_ref, omla_ref, onsa_ref,
     kc_s, kr_s, ks_s, vs_s, m1_ref, l1_ref, acc1_ref, m2_ref, l2_ref, acc2_ref) = refs[4 * pps:]
    c = pl.program_id(1)
    qp = QUERY_PAD
    jj = NSA_HPG
    hh = MLA_HEADS
    kk = pps * PAGE_ROWS

    @pl.when(c == 0)
    def _():
        m1_ref[...] = jnp.full(m1_ref.shape, NEG_INF, F32)
        l1_ref[...] = jnp.zeros_like(l1_ref)
        acc1_ref[...] = jnp.zeros_like(acc1_ref)
        m2_ref[...] = jnp.full(m2_ref.shape, NEG_INF, F32)
        l2_ref[...] = jnp.zeros_like(l2_ref)
        acc2_ref[...] = jnp.zeros_like(acc2_ref)

    for p in range(pps):
        sl = slice(p * PAGE_ROWS, (p + 1) * PAGE_ROWS)
        sl2 = slice(p * PAGE_IROWS, (p + 1) * PAGE_IROWS)
        kc_s[sl, :] = ckv_p[p][...].astype(BF16)
        kr_s[:, sl] = kr_p[p][...].astype(BF16)
        ks_s[sl2, :] = ks_p[p][...].astype(BF16)
        vs_s[sl2, :] = vs_p[p][...].astype(BF16)

    def online(m_ref, l_ref, acc_ref, s, v):
        m_prev = m_ref[...]
        m_new = jnp.maximum(m_prev, jnp.max(s, axis=-1, keepdims=True))
        alpha = jnp.exp2(m_prev - m_new)
        p = jnp.exp2(s - m_new)
        l_ref[...] = alpha * l_ref[...] + jnp.sum(p, axis=-1, keepdims=True)
        acc_ref[...] = alpha * acc_ref[...] + _dot(p.astype(BF16), v)
        m_ref[...] = m_new

    ql = ql_ref[...]
    qr = qr_ref[...]
    kc = kc_s[...]
    s1 = _dot_nt(ql, kc) + _dot(qr, kr_s[...])
    online(m1_ref, l1_ref, acc1_ref, s1, kc)

    gg = NSA_KV_GROUPS
    qn = qn_ref[...]
    chosen = _dot(selc_ref[...].astype(BF16), e_ref[...])
    s2 = _dot_nt(qn, ks_s[...]).reshape(gg, jj, qp, gg * kk)
    s2 = jnp.where(chosen.reshape(gg, 1, qp, gg * kk) > 0.5, s2, NEG_INF).reshape(gg * jj * qp, gg * kk)
    online(m2_ref, l2_ref, acc2_ref, s2, vs_s[...])

    @pl.when(c == nch - 1)
    def _():
        npad = NEW_PAD
        kidx = lax.broadcasted_iota(jnp.int32, (1, qp, npad), 2)
        qidx = lax.broadcasted_iota(jnp.int32, (1, qp, npad), 1)
        new_ok = (kidx <= qidx) & (kidx < nq)
        ckvn = ckvn_ref[...]
        s1n = _dot_nt(ql, ckvn) + _dot_nt(qr, krn_ref[...])
        s1n = jnp.where(new_ok, s1n.reshape(hh, qp, npad), NEG_INF).reshape(hh * qp, npad)
        online(m1_ref, l1_ref, acc1_ref, s1n, ckvn)
        o_lat = (acc1_ref[...] / l1_ref[...]).astype(BF16)
        for h in range(hh):
            omla_ref[:, h * MLA_V:(h + 1) * MLA_V] = _dot(o_lat[h * qp:(h + 1) * qp], wuv_ref[h]).astype(omla_ref.dtype)

        wl = kw_ref.shape[0]
        widx = lax.broadcasted_iota(jnp.int32, (1, qp, wl), 2)
        wpos = past - win_buf + widx
        tw = past + jnp.minimum(lax.broadcasted_iota(jnp.int32, (1, qp, wl), 1), nq - 1)
        win_ok = (widx < win_buf + nq) & (wpos >= tw - WINDOW) & (wpos <= tw)
        shp = (gg, 1, qp, gg * npad)
        ncol = lax.broadcasted_iota(jnp.int32, shp, 3)
        nrow_g = lax.broadcasted_iota(jnp.int32, shp, 0)
        nq_i = lax.broadcasted_iota(jnp.int32, shp, 2)
        nkk = ncol & (npad - 1)
        sel_new = (((ncol >> NEW_SHIFT) == nrow_g) & (nkk <= nq_i) & (nkk < nq)
                   & (sell_ref[:, 0:gg * npad].reshape(shp) > 0.5))
        s2n = _dot_nt(qn, ksn_ref[...]).reshape(gg, jj, qp, gg * npad)
        s2n = jnp.where(sel_new, s2n, NEG_INF).reshape(gg * jj * qp, gg * npad)
        online(m2_ref, l2_ref, acc2_ref, s2n, vsn_ref[...])
        o_s_all = acc2_ref[...] / l2_ref[...]
        for g in range(NSA_KV_GROUPS):
            cols = slice(g * NSA_DH, (g + 1) * NSA_DH)
            rws = slice(g * jj * qp, (g + 1) * jj * qp)
            qg = qn[rws]
            o_s = o_s_all[rws]

            sw = _dot_nt(qg, kw_ref[:, cols]).reshape(jj, qp, wl)
            sw = jnp.where(win_ok, sw, NEG_INF).reshape(jj * qp, wl)
            pw = jnp.exp2(sw - jnp.max(sw, axis=-1, keepdims=True))
            pw = pw / jnp.sum(pw, axis=-1, keepdims=True)
            o_w = _dot(pw.astype(BF16), vw_ref[:, cols])

            gt = gt_ref[rws, :]
            o = gt[:, 0:1] * oc_ref[rws, :] + gt[:, 1:2] * o_s + gt[:, 2:3] * o_w
            for j in range(jj):
                hcol = (g * jj + j) * NSA_DH
                onsa_ref[:, hcol:hcol + NSA_DH] = o[j * qp:(j + 1) * qp].astype(onsa_ref.dtype)


def _sample_attend(page_table, ckv_pool, kr_pool, ks_pool, vs_pool, ql, qr, qn, selc, sell, oc, gt,
                   ckvn, krn, ksn, vsn, kw, vw, expand, wuv, past, nq, win_buf):
    db, npages = page_table.shape
    pps = min(PAGES_PER_STEP, npages)
    nch = npages // pps
    kk = pps * PAGE_ROWS
    qp = QUERY_PAD
    hh = MLA_HEADS
    rows = NSA_HPG * qp

    def pages(rows_, width):
        return [pl.BlockSpec((None, rows_, width),
                             lambda b, c, pt, p=p: (pt[b * npages + c * pps + p], 0, 0)) for p in range(pps)]

    def per_seq(shape):
        nd = len(shape)
        return pl.BlockSpec((None,) + tuple(shape), lambda b, c, pt: (b,) + (0,) * nd)

    def const(shape):
        nd = len(shape)
        return pl.BlockSpec(tuple(shape), lambda b, c, pt: (0,) * nd)

    in_specs = (
        pages(PAGE_ROWS, MLA_KV_RANK) + pages(MLA_ROPE, PAGE_ROWS)
        + pages(PAGE_IROWS, NSA_DH) + pages(PAGE_IROWS, NSA_DH) + [
            per_seq((hh * qp, MLA_KV_RANK)),
            per_seq((hh * qp, MLA_ROPE)),
            per_seq((NSA_KV_GROUPS * rows, NSA_DH)),
            pl.BlockSpec((None, None, NSA_KV_GROUPS * qp, LANE), lambda b, c, pt: (b, c, 0, 0)),
            per_seq((NSA_KV_GROUPS * qp, LANE)),
            per_seq((NSA_KV_GROUPS * rows, NSA_DH)),
            per_seq((NSA_KV_GROUPS * rows, 3)),
            per_seq((NEW_PAD, MLA_KV_RANK)),
            per_seq((NEW_PAD, MLA_ROPE)),
            per_seq((NSA_KV_GROUPS * NEW_PAD, NSA_DH)),
            per_seq((NSA_KV_GROUPS * NEW_PAD, NSA_DH)),
            per_seq(kw.shape[1:]),
            per_seq(vw.shape[1:]),
            const(expand.shape),
            const(wuv.shape),
        ])
    grid_spec = pltpu.PrefetchScalarGridSpec(
        num_scalar_prefetch=1,
        grid=(db, nch),
        in_specs=in_specs,
        out_specs=[per_seq((qp, hh * MLA_V)), per_seq((qp, NSA_HEADS * NSA_DH))],
        scratch_shapes=[
            pltpu.VMEM((kk, MLA_KV_RANK), BF16),
            pltpu.VMEM((MLA_ROPE, kk), BF16),
            pltpu.VMEM((NSA_KV_GROUPS * kk, NSA_DH), BF16),
            pltpu.VMEM((NSA_KV_GROUPS * kk, NSA_DH), BF16),
            pltpu.VMEM((hh * qp, 1), F32),
            pltpu.VMEM((hh * qp, 1), F32),
            pltpu.VMEM((hh * qp, MLA_KV_RANK), F32),
            pltpu.VMEM((NSA_KV_GROUPS * rows, 1), F32),
            pltpu.VMEM((NSA_KV_GROUPS * rows, 1), F32),
            pltpu.VMEM((NSA_KV_GROUPS * rows, NSA_DH), F32),
        ],
    )
    return pl.pallas_call(
        functools.partial(_sample_attend_kernel, pps=pps, nch=nch, past=past, nq=nq, win_buf=win_buf),
        grid_spec=grid_spec,
        out_shape=[
            jax.ShapeDtypeStruct((db, qp, hh * MLA_V), BF16),
            jax.ShapeDtypeStruct((db, qp, NSA_HEADS * NSA_DH), BF16),
        ],
        compiler_params=_params("parallel", "arbitrary"),
        name="sample_attend",
    )(page_table.reshape(-1), *([ckv_pool] * pps), *([kr_pool] * pps), *([ks_pool] * pps), *([vs_pool] * pps),
      ql, qr, qn, selc, sell, oc, gt, ckvn, krn, ksn, vsn, kw, vw, expand, wuv)


def _outproj_kernel(x_ref, a_ref, b_ref, wa_ref, wb_ref, o_ref):
    o_ref[...] = x_ref[...] + _dot(a_ref[...], wa_ref[...]) + _dot(b_ref[...], wb_ref[...])


def _outproj(x, a, b, wa, wb):
    n, d = x.shape
    tm = _row_tile(n, 512)
    return pl.pallas_call(
        _outproj_kernel,
        grid=(n // tm,),
        in_specs=[
            pl.BlockSpec((tm, d), lambda i: (i, 0)),
            pl.BlockSpec((tm, a.shape[1]), lambda i: (i, 0)),
            pl.BlockSpec((tm, b.shape[1]), lambda i: (i, 0)),
            pl.BlockSpec(wa.shape, lambda i: (0, 0)),
            pl.BlockSpec(wb.shape, lambda i: (0, 0)),
        ],
        out_specs=pl.BlockSpec((tm, d), lambda i: (i, 0)),
        out_shape=jax.ShapeDtypeStruct((n, d), F32),
        compiler_params=_params("parallel"),
        name="outproj",
    )(x, a, b, wa, wb)


def _rope_tables(pos, rot_dim):
    inv = ROPE_THETA ** (-jnp.arange(0, rot_dim, 2, dtype=F32) / rot_dim)
    ang = pos.astype(F32)[:, None] * inv[None, :]
    return jnp.cos(ang), jnp.sin(ang)


def _rope(x, cos, sin):
    half = cos.shape[-1]
    c = cos[:, None, :]
    s = sin[:, None, :]
    x1, x2 = x[..., :half], x[..., half:2 * half]
    return jnp.concatenate([x1 * c - x2 * s, x2 * c + x1 * s, x[..., 2 * half:]], axis=-1)


def _pad_queries(a, db, nq):
    heads, d = a.shape[1:]
    a = a.reshape(db, nq, heads, d).transpose(0, 2, 1, 3)
    a = jnp.pad(a, ((0, 0), (0, 0), (0, QUERY_PAD - nq), (0, 0)))
    return a.reshape(db, heads * QUERY_PAD, d)


def _pad_new(a, db, nq):
    a = a.reshape(db, nq, a.shape[-1])
    return jnp.pad(a, ((0, 0), (0, NEW_PAD - nq), (0, 0))).astype(BF16)


def _pad_new_groups(a, db, nq):
    a = a.reshape(db, nq, NSA_KV_GROUPS, NSA_DH).transpose(0, 2, 1, 3)
    a = jnp.pad(a, ((0, 0), (0, 0), (0, NEW_PAD - nq), (0, 0)))
    return a.reshape(db, NSA_KV_GROUPS * NEW_PAD, NSA_DH).astype(BF16)


def kernel(x_prompt, x_sample, cache_mla_ckv, cache_mla_krope, cache_nsa_k_cmp, cache_nsa_v_cmp, cache_nsa_k_sel, cache_nsa_v_sel, state_nsa_k_win, state_nsa_v_win, page_table, ffn1_norm, w_ffn1_gate, w_ffn1_up, w_ffn1_down, mix_norm, w_in, mla_q_norm, w_mla_q_up, mla_kv_norm, w_mla_k_up, w_mla_v_up, nsa_cmp_pos_k, nsa_cmp_lin_k, nsa_cmp_pos_v, nsa_cmp_lin_v, w_out, ffn2_norm, w_ffn2_gate, w_ffn2_up, w_ffn2_down, final_norm):
    nb, seq, d = x_prompt.shape
    db, nq = x_sample.shape[:2]
    depth, n_pool = cache_mla_ckv.shape[:2]
    npages = page_table.shape[1]
    past = npages * PAGE_ROWS
    win_buf = state_nsa_k_win.shape[2]
    np_tok = nb * seq
    ns_tok = db * nq
    assert nq <= QUERY_PAD and cache_mla_ckv.shape[2] == PAGE_ROWS and seq % QUERY_BLOCK == 0
    hh, gg, dh = MLA_HEADS, NSA_KV_GROUPS, NSA_DH
    x = jnp.concatenate([x_prompt.reshape(np_tok, d), x_sample.reshape(ns_tok, d)], axis=0)
    pos = jnp.concatenate([jnp.tile(jnp.arange(seq), nb), jnp.tile(past + jnp.arange(nq), db)])
    cos_m, sin_m = _rope_tables(pos, MLA_ROPE)
    cos_n, sin_n = _rope_tables(pos, NSA_ROT)

    ns_s = -(-(past + nq) // SEL_BLOCK)
    selw = -(-ns_s // LANE) * LANE
    nseg_s = past // CMP_STRIDE
    c_st = np.arange(nseg_s)[:, None] * CMP_STRIDE
    s_st = np.arange(selw)[None, :] * SEL_BLOCK
    overlap = jnp.asarray(((c_st < s_st + SEL_BLOCK) & (c_st + CMP_BLOCK > s_st)), dtype=BF16)
    pps = min(PAGES_PER_STEP, npages)
    nch = npages // pps
    bpc = pps * PAGE_ROWS // SEL_BLOCK
    assert gg * bpc <= LANE
    e_row = np.arange(LANE)[:, None]
    e_col = np.arange(gg * pps * PAGE_ROWS)[None, :]
    expand = jnp.asarray((e_row // bpc == e_col % gg) & (e_row % bpc == e_col // (gg * SEL_BLOCK)), dtype=BF16)

    rows_p = [[] for _ in range(8)]
    rows_s = [[] for _ in range(8)]
    for l in range(depth):
        wq = w_mla_q_up[l].reshape(MLA_Q_RANK, hh, MLA_NOPE + MLA_ROPE)
        wqn = wq[:, :, :MLA_NOPE].reshape(MLA_Q_RANK, hh * MLA_NOPE).astype(BF16)
        wqr = wq[:, :, MLA_NOPE:].reshape(MLA_Q_RANK, hh * MLA_ROPE).astype(BF16)
        wuk = w_mla_k_up[l].reshape(MLA_KV_RANK, hh, MLA_NOPE).transpose(1, 2, 0).astype(BF16)
        wuv = w_mla_v_up[l].reshape(MLA_KV_RANK, hh, MLA_V).transpose(1, 0, 2).astype(BF16)
        wi = w_in[l]
        o_q, o_kv, o_kr = 0, MLA_Q_RANK, MLA_Q_RANK + MLA_KV_RANK
        o_qn = o_kr + MLA_ROPE
        o_kv6 = o_qn + NSA_HEADS * dh
        o_g = o_kv6 + 6 * KV_COLS
        w_in_p = jnp.concatenate([
            wi[:, o_q:o_kr], wi[:, o_qn:o_g], wi[:, o_kr:o_qn], wi[:, o_g:],
            jnp.zeros((d, IN_PAD - wi.shape[1]), wi.dtype)], axis=1).astype(BF16)
        pwk, pwv = (jnp.repeat(w, NSA_KV_GROUPS, axis=0) for w in (nsa_cmp_pos_k[l], nsa_cmp_pos_v[l]))
        link = nsa_cmp_lin_k[l].astype(BF16)
        linv = nsa_cmp_lin_v[l].astype(BF16)

        x = _ffn(x, ffn1_norm[l], w_ffn1_gate[l].astype(BF16), w_ffn1_up[l].astype(BF16),
                 w_ffn1_down[l].astype(BF16))
        z = _inproj(x, mix_norm[l], w_in_p)
        ql, qr_raw, ckv = _mlaprep(z, mla_q_norm[l], mla_kv_norm[l], wqn, wqr, wuk)

        qr = _rope(qr_raw.reshape(-1, hh, MLA_ROPE), cos_m, sin_m) * MLA_QSCALE
        krope = _rope(z[:, None, COL_KR:COL_KR + MLA_ROPE], cos_m, sin_m)[:, 0]
        qn = _rope(z[:, COL_QN:COL_KC].reshape(-1, NSA_HEADS, dh), cos_n, sin_n) * NSA_QSCALE

        def kv_rot(col):
            return _rope(z[:, col:col + KV_COLS].reshape(-1, gg, dh), cos_n, sin_n).reshape(-1, KV_COLS)

        k_cmp, k_sel, k_win = kv_rot(COL_KC), kv_rot(COL_KS), kv_rot(COL_KW)
        v_cmp, v_sel, v_win = (z[:, c0:c0 + KV_COLS] for c0 in (COL_VC, COL_VS, COL_VW))
        ckv_b, krope_b = ckv.astype(BF16), krope.astype(BF16)
        k_sel_b, v_sel_b, k_win_b, v_win_b = (a.astype(BF16) for a in (k_sel, v_sel, k_win, v_win))

        qr_t = qr.transpose(1, 0, 2).astype(BF16)
        qn_t = qn.transpose(1, 0, 2).astype(BF16)
        o_mla_p = _mla_prompt(ql, qr_t, ckv_b, krope_b, wuv, nb, seq)
        kcmp_p, vcmp_p = _compress_prompt(k_cmp, v_cmp, pwk, pwv, link, linv, nb, seq)
        o_nsa_p = _nsa_prompt(qn_t, kcmp_p, vcmp_p, k_sel_b, v_sel_b, k_win_b, v_win_b, z, nb, seq)

        sm = slice(np_tok, None)
        ql_s = _pad_queries(ql[:, sm].transpose(1, 0, 2), db, nq)
        qr_s = _pad_queries(qr[sm], db, nq).astype(BF16)
        qn_s = _pad_queries(qn[sm], db, nq).astype(BF16)
        gates = jax.nn.sigmoid(z[sm, COL_G:COL_G + GATE_COLS]).reshape(ns_tok, NSA_HEADS, 3)
        gt_s = _pad_queries(gates, db, nq)
        ckv_pool = cache_mla_ckv[l]
        kr_pool = jnp.swapaxes(cache_mla_krope[l], 1, 2)
        kc_pool, vc_pool, ks_pool, vs_pool = (c[l].reshape(n_pool, PAGE_IROWS, dh) for c in (
            cache_nsa_k_cmp, cache_nsa_v_cmp, cache_nsa_k_sel, cache_nsa_v_sel))
        o_c, sel = _sample_select(page_table, kc_pool, vc_pool, qn_s.reshape(db, gg, NSA_HPG * QUERY_PAD, dh),
                                  pwk, pwv, link, linv, overlap, past, nq, ns_s, min(SEL_TOPK, ns_s))
        selc = sel[:, :, :nch * bpc].reshape(db, gg, QUERY_PAD, nch, bpc).transpose(0, 3, 1, 2, 4)
        selc = selc[:, :, :, :, None, :] * jnp.eye(gg, dtype=F32)[None, None, :, None, :, None]
        selc = selc.reshape(db, nch, gg * QUERY_PAD, gg * bpc)
        selc = jnp.pad(selc, ((0, 0), (0, 0), (0, 0), (0, LANE - gg * bpc)))
        sell = jnp.broadcast_to(sel[:, :, nch * bpc:nch * bpc + 1], (db, gg * QUERY_PAD, LANE))
        kwin_full = jnp.concatenate([state_nsa_k_win[l].reshape(db, win_buf, KV_COLS),
                                     k_win[sm].reshape(db, nq, KV_COLS)], axis=1)
        vwin_full = jnp.concatenate([state_nsa_v_win[l].reshape(db, win_buf, KV_COLS),
                                     v_win[sm].reshape(db, nq, KV_COLS)], axis=1)
        wpad = ((0, 0), (0, NEW_PAD - nq), (0, 0))
        o_mla_s, o_nsa_s = _sample_attend(
            page_table, ckv_pool, kr_pool, ks_pool, vs_pool, ql_s, qr_s, qn_s, selc, sell,
            o_c.reshape(db, -1, dh), gt_s,
            _pad_new(ckv[sm], db, nq), _pad_new(krope[sm], db, nq), _pad_new_groups(k_sel[sm], db, nq),
            _pad_new_groups(v_sel[sm], db, nq), jnp.pad(kwin_full, wpad).astype(BF16),
            jnp.pad(vwin_full, wpad).astype(BF16), expand, wuv, past, nq, win_buf)

        mix_a = jnp.concatenate([o_mla_p, o_mla_s[:, :nq].reshape(ns_tok, -1)], axis=0)
        mix_b = jnp.concatenate([o_nsa_p, o_nsa_s[:, :nq].reshape(ns_tok, -1)], axis=0)
        wo = w_out[l].astype(BF16)
        x = _outproj(x, mix_a, mix_b, wo[:hh * MLA_V], wo[hh * MLA_V:])
        x = _ffn(x, ffn2_norm[l], w_ffn2_gate[l].astype(BF16), w_ffn2_up[l].astype(BF16),
                 w_ffn2_down[l].astype(BF16), final_g=final_norm if l == depth - 1 else None)

        win_p = min(WINDOW, seq)
        new = (ckv, krope, k_cmp, v_cmp, k_sel, v_sel)
        for i, a in enumerate(new):
            tail = a.shape[1:] if i < 2 else (gg, dh)
            rows_p[i].append(a[:np_tok].reshape((nb, seq) + tail))
            rows_s[i].append(a[np_tok:].reshape((db, nq) + tail))
        rows_p[6].append(k_win[:np_tok].reshape(nb, seq, gg, dh)[:, -win_p:])
        rows_p[7].append(v_win[:np_tok].reshape(nb, seq, gg, dh)[:, -win_p:])
        rows_s[6].append(kwin_full[:, -win_buf:].reshape(db, win_buf, gg, dh))
        rows_s[7].append(vwin_full[:, -win_buf:].reshape(db, win_buf, gg, dh))

    y_prompt = x[:np_tok].reshape(nb, seq, d)
    y_sample = x[np_tok:].reshape(db, nq, d)
    out = [y_prompt, y_sample]
    for i in range(8):
        out += [jnp.stack(rows_p[i]), jnp.stack(rows_s[i])]
    return tuple(out)
```

```python
import functools

import jax
import jax.numpy as jnp
import numpy as np
from jax import lax
from jax.experimental import pallas as pl
from jax.experimental.pallas import tpu as pltpu

MLA_HEADS = 8
MLA_NOPE = 128
MLA_ROPE = 64
MLA_V = 128
MLA_Q_RANK = 512
MLA_KV_RANK = 256
NSA_HEADS = 8
NSA_KV_GROUPS = 2
NSA_HPG = NSA_HEADS // NSA_KV_GROUPS
NSA_DH = 128
CMP_BLOCK = 32
CMP_STRIDE = 16
SEL_BLOCK = 64
SEL_TOPK = 16
SEL_LOCAL = 2
WINDOW = 512
ROPE_THETA = 500000.0
NSA_ROT = NSA_DH // 4
RMS_EPS = 1e-6
QUERY_BLOCK = 128
MLA_SCALE = (MLA_NOPE + MLA_ROPE) ** -0.5
NSA_SCALE = NSA_DH ** -0.5
LOG2E = 1.4426950408889634
MLA_QSCALE = MLA_SCALE * LOG2E
NSA_QSCALE = NSA_SCALE * LOG2E
NEG_INF = -1e30
BIG = 1e6
TINY = 1e-30
KV_COLS = NSA_KV_GROUPS * NSA_DH
GATE_COLS = 3 * NSA_HEADS

COL_Q = 0
COL_KV = COL_Q + MLA_Q_RANK
COL_QN = COL_KV + MLA_KV_RANK
COL_KC = COL_QN + NSA_HEADS * NSA_DH
COL_VC = COL_KC + KV_COLS
COL_KS = COL_VC + KV_COLS
COL_VS = COL_KS + KV_COLS
COL_KW = COL_VS + KV_COLS
COL_VW = COL_KW + KV_COLS
COL_KR = COL_VW + KV_COLS
COL_G = COL_KR + MLA_ROPE
LANE = 128
IN_PAD = -(-(COL_G + GATE_COLS) // (3 * LANE)) * (3 * LANE)

QUERY_PAD = 8
NEW_PAD = 16
PAGE_ROWS = 128
SEL_SHIFT = SEL_BLOCK.bit_length() - 1
NEW_SHIFT = NEW_PAD.bit_length() - 1
PAGES_PER_STEP = 16
MLA_KEY_CHUNK = 1024
MLA_HEAD_GROUP = 4
NSA_KEY_CHUNK = 1024
NSA_HEAD_GROUP = 4
VMEM_LIMIT = 56 * 1024 * 1024

F32 = jnp.float32
BF16 = jnp.bfloat16


def _dot(a, b):
    return jnp.dot(a, b, preferred_element_type=F32)


def _dot_nt(a, b):
    return lax.dot_general(a, b, (((1,), (1,)), ((), ())), preferred_element_type=F32)


def _rms(x, g):
    return x * lax.rsqrt(jnp.mean(x * x, axis=-1, keepdims=True) + RMS_EPS) * g


def _split3_dot_nt(w_bf16, x):
    hi = x.astype(BF16)
    r1 = x - hi.astype(F32)
    mid = r1.astype(BF16)
    lo = (r1 - mid.astype(F32)).astype(BF16)
    return _dot_nt(w_bf16, hi) + _dot_nt(w_bf16, mid) + _dot_nt(w_bf16, lo)


def _split3_dot(x, w_bf16):
    hi = x.astype(BF16)
    r1 = x - hi.astype(F32)
    mid = r1.astype(BF16)
    lo = (r1 - mid.astype(F32)).astype(BF16)
    return _dot(hi, w_bf16) + _dot(mid, w_bf16) + _dot(lo, w_bf16)


def _row_tile(n, want):
    t = min(want, n)
    while n % t:
        t -= 8
    return t


def _params(*sem):
    return pltpu.CompilerParams(dimension_semantics=sem, vmem_limit_bytes=VMEM_LIMIT)


def _ffn_kernel(x_ref, g_ref, wg_ref, wu_ref, wd_ref, *rest, final):
    if final:
        fg_ref, o_ref, h_ref, acc_ref = rest
    else:
        o_ref, h_ref, acc_ref = rest
    j = pl.program_id(1)

    @pl.when(j == 0)
    def _():
        h_ref[...] = _rms(x_ref[...], g_ref[...]).astype(BF16)
        acc_ref[...] = jnp.zeros_like(acc_ref)

    h = h_ref[...]
    a = _dot(h, wg_ref[...])
    u = _dot(h, wu_ref[...])
    act = (a / (1.0 + jnp.exp(-a))) * u
    acc_ref[...] += _dot(act.astype(BF16), wd_ref[...])

    @pl.when(j == pl.num_programs(1) - 1)
    def _():
        y = x_ref[...] + 0.5 * acc_ref[...]
        if final:
            y = _rms(y, fg_ref[...])
        o_ref[...] = y


def _ffn(x, g, wg, wu, wd, final_g=None):
    n, d = x.shape
    dff = wg.shape[1]
    tm = _row_tile(n, 512)
    tf = 512 if dff % 512 == 0 else dff
    in_specs = [
        pl.BlockSpec((tm, d), lambda i, j: (i, 0)),
        pl.BlockSpec((1, d), lambda i, j: (0, 0)),
        pl.BlockSpec((d, tf), lambda i, j: (0, j)),
        pl.BlockSpec((d, tf), lambda i, j: (0, j)),
        pl.BlockSpec((tf, d), lambda i, j: (j, 0)),
    ]
    args = [x, g.reshape(1, d), wg, wu, wd]
    if final_g is not None:
        in_specs.append(pl.BlockSpec((1, d), lambda i, j: (0, 0)))
        args.append(final_g.reshape(1, d))
    return pl.pallas_call(
        functools.partial(_ffn_kernel, final=final_g is not None),
        grid=(n // tm, dff // tf),
        in_specs=in_specs,
        out_specs=pl.BlockSpec((tm, d), lambda i, j: (i, 0)),
        out_shape=jax.ShapeDtypeStruct((n, d), F32),
        scratch_shapes=[pltpu.VMEM((tm, d), BF16), pltpu.VMEM((tm, d), F32)],
        compiler_params=_params("parallel", "arbitrary"),
        name="ffn",
    )(*args)


def _inproj_kernel(x_ref, g_ref, w_ref, o_ref, h_ref):
    @pl.when(pl.program_id(1) == 0)
    def _():
        h_ref[...] = _rms(x_ref[...], g_ref[...]).astype(BF16)

    o_ref[...] = _dot(h_ref[...], w_ref[...])


def _inproj(x, g, w):
    n, d = x.shape
    nout = w.shape[1]
    tm = _row_tile(n, 512)
    tn = nout // 3
    return pl.pallas_call(
        _inproj_kernel,
        grid=(n // tm, nout // tn),
        in_specs=[
            pl.BlockSpec((tm, d), lambda i, j: (i, 0)),
            pl.BlockSpec((1, d), lambda i, j: (0, 0)),
            pl.BlockSpec((d, tn), lambda i, j: (0, j)),
        ],
        out_specs=pl.BlockSpec((tm, tn), lambda i, j: (i, j)),
        out_shape=jax.ShapeDtypeStruct((n, nout), F32),
        scratch_shapes=[pltpu.VMEM((tm, d), BF16)],
        compiler_params=_params("parallel", "arbitrary"),
        name="inproj",
    )(x, g.reshape(1, d), w)


def _mlaprep_kernel(zq_ref, zkv_ref, gq_ref, gkv_ref, wqn_ref, wqr_ref, wuk_ref, ql_ref, qr_ref, ckv_ref):
    cq = _rms(zq_ref[...], gq_ref[...]).astype(BF16)
    qn = _dot(cq, wqn_ref[...]).astype(BF16)
    for h in range(MLA_HEADS):
        ql_ref[h] = (_dot(qn[:, h * MLA_NOPE:(h + 1) * MLA_NOPE], wuk_ref[h]) * MLA_QSCALE).astype(BF16)
    qr_ref[...] = _dot(cq, wqr_ref[...])
    ckv_ref[...] = _rms(zkv_ref[...], gkv_ref[...])


def _mlaprep(z, gq, gkv, wqn, wqr, wuk):
    n = z.shape[0]
    tm = _row_tile(n, 512)
    return pl.pallas_call(
        _mlaprep_kernel,
        grid=(n // tm,),
        in_specs=[
            pl.BlockSpec((tm, MLA_Q_RANK), lambda i: (i, COL_Q // MLA_Q_RANK)),
            pl.BlockSpec((tm, MLA_KV_RANK), lambda i: (i, COL_KV // MLA_KV_RANK)),
            pl.BlockSpec((1, MLA_Q_RANK), lambda i: (0, 0)),
            pl.BlockSpec((1, MLA_KV_RANK), lambda i: (0, 0)),
            pl.BlockSpec(wqn.shape, lambda i: (0, 0)),
            pl.BlockSpec(wqr.shape, lambda i: (0, 0)),
            pl.BlockSpec(wuk.shape, lambda i: (0, 0, 0)),
        ],
        out_specs=[
            pl.BlockSpec((MLA_HEADS, tm, MLA_KV_RANK), lambda i: (0, i, 0)),
            pl.BlockSpec((tm, MLA_HEADS * MLA_ROPE), lambda i: (i, 0)),
            pl.BlockSpec((tm, MLA_KV_RANK), lambda i: (i, 0)),
        ],
        out_shape=[
            jax.ShapeDtypeStruct((MLA_HEADS, n, MLA_KV_RANK), BF16),
            jax.ShapeDtypeStruct((n, MLA_HEADS * MLA_ROPE), F32),
            jax.ShapeDtypeStruct((n, MLA_KV_RANK), F32),
        ],
        compiler_params=_params("parallel"),
        name="mlaprep",
    )(z, z, gq.reshape(1, -1), gkv.reshape(1, -1), wqn, wqr, wuk)


def _mla_prompt_kernel(ql_ref, qr_ref, ckv_ref, kr_ref, wuv_ref, o_ref, m_ref, l_ref, acc_ref, *, tq, kc):
    q0 = pl.program_id(1) * tq
    hh = MLA_HEADS
    m_ref[...] = jnp.full(m_ref.shape, NEG_INF, F32)
    l_ref[...] = jnp.zeros_like(l_ref)
    acc_ref[...] = jnp.zeros_like(acc_ref)
    t = q0 + lax.broadcasted_iota(jnp.int32, (tq, kc), 0)
    koff = lax.broadcasted_iota(jnp.int32, (tq, kc), 1)

    def chunk(c, masked):
        k0 = pl.multiple_of(c * kc, kc)
        kk = ckv_ref[pl.ds(k0, kc), :]
        kr = kr_ref[pl.ds(k0, kc), :]
        hg = MLA_HEAD_GROUP
        for h0 in range(0, hh, hg):
            hs = slice(h0, h0 + hg)
            ql = ql_ref[hs].reshape(hg * tq, MLA_KV_RANK)
            qr = qr_ref[hs].reshape(hg * tq, MLA_ROPE)
            s = (_dot_nt(ql, kk) + _dot_nt(qr, kr)).reshape(hg, tq, kc)
            if masked:
                s = jnp.where(koff + k0 <= t, s, NEG_INF)
            m_prev = m_ref[hs]
            m_new = jnp.maximum(m_prev, jnp.max(s, axis=-1, keepdims=True))
            alpha = jnp.exp2(m_prev - m_new)
            p = jnp.exp2(s - m_new)
            l_ref[hs] = alpha * l_ref[hs] + jnp.sum(p, axis=-1, keepdims=True)
            pv = _dot(p.reshape(hg * tq, kc).astype(BF16), kk)
            acc_ref[hs] = alpha * acc_ref[hs] + pv.reshape(hg, tq, MLA_KV_RANK)
            m_ref[hs] = m_new

    def body(c, carry):
        chunk(c, False)
        return carry

    lax.fori_loop(0, q0 // kc, body, 0)
    chunk(q0 // kc, True)
    for h in range(hh):
        o = (acc_ref[h] / l_ref[h]).astype(BF16)
        o_ref[:, h * MLA_V:(h + 1) * MLA_V] = _dot(o, wuv_ref[h]).astype(o_ref.dtype)


def _mla_prompt(ql, qr, ckv, kr, wuv, nb, seq):
    tq = min(QUERY_BLOCK, seq)
    kc = min(MLA_KEY_CHUNK, seq)
    assert kc % tq == 0
    nq = seq // tq
    hh = MLA_HEADS
    return pl.pallas_call(
        functools.partial(_mla_prompt_kernel, tq=tq, kc=kc),
        grid=(nb, nq),
        in_specs=[
            pl.BlockSpec((hh, tq, MLA_KV_RANK), lambda b, i: (0, b * nq + i, 0)),
            pl.BlockSpec((hh, tq, MLA_ROPE), lambda b, i: (0, b * nq + i, 0)),
            pl.BlockSpec((seq, MLA_KV_RANK), lambda b, i: (b, 0)),
            pl.BlockSpec((seq, MLA_ROPE), lambda b, i: (b, 0)),
            pl.BlockSpec(wuv.shape, lambda b, i: (0, 0, 0)),
        ],
        out_specs=pl.BlockSpec((tq, hh * MLA_V), lambda b, i: (b * nq + i, 0)),
        out_shape=jax.ShapeDtypeStruct((nb * seq, hh * MLA_V), BF16),
        scratch_shapes=[
            pltpu.VMEM((hh, tq, 1), F32),
            pltpu.VMEM((hh, tq, 1), F32),
            pltpu.VMEM((hh, tq, MLA_KV_RANK), F32),
        ],
        compiler_params=_params("parallel", "arbitrary"),
        name="mla_prompt",
    )(ql, qr, ckv, kr, wuv)


SUBLANES = 8
TILES_PER_SEG = NSA_KV_GROUPS * CMP_STRIDE // SUBLANES
SEGS_PER_PAGE = PAGE_ROWS // CMP_STRIDE
PAGE_IROWS = NSA_KV_GROUPS * PAGE_ROWS


def _page_segment_sums(x_ref, row0, wt_ref):
    sub = lax.broadcasted_iota(jnp.int32, (SUBLANES, NSA_DH), 0)
    out = [[jnp.zeros((SEGS_PER_PAGE, NSA_DH), F32) for _ in range(NSA_KV_GROUPS)] for _ in range(2)]
    for n in range(SEGS_PER_PAGE):
        tiles = [x_ref[pl.ds(row0 + (n * TILES_PER_SEG + k) * SUBLANES, SUBLANES), :] for k in range(TILES_PER_SEG)]
        for m in range(2):
            w0 = m * TILES_PER_SEG * SUBLANES
            p = tiles[0] * wt_ref[w0:w0 + SUBLANES, :]
            for k in range(1, TILES_PER_SEG):
                p = p + tiles[k] * wt_ref[w0 + k * SUBLANES:w0 + (k + 1) * SUBLANES, :]
            p = p + pltpu.roll(p, 4, 0)
            p = p + pltpu.roll(p, 2, 0)
            q = pltpu.roll(p, 1, 0)
            for g in range(NSA_KV_GROUPS):
                out[m][g] = jnp.where(sub == n, p if n % 2 == g else q, out[m][g])
    return out


def _compress_prompt_kernel(k_ref, v_ref, wtk_ref, wtv_ref, link_ref, linv_ref, kc_ref, vc_ref, *, nseg):
    npg = nseg // SEGS_PER_PAGE
    for x_ref, wt_ref, lin_ref, o_ref in ((k_ref, wtk_ref, link_ref, kc_ref), (v_ref, wtv_ref, linv_ref, vc_ref)):
        parts = [_page_segment_sums(x_ref, pg * PAGE_IROWS, wt_ref) for pg in range(npg)]
        for g in range(NSA_KV_GROUPS):
            a0 = jnp.concatenate([pt[0][g] for pt in parts], axis=0)
            a1 = jnp.concatenate([pt[1][g] for pt in parts], axis=0)
            acc = (a0 + pltpu.roll(a1, nseg - 1, 0)).astype(BF16)
            o_ref[g] = _dot(acc, lin_ref[...]).astype(BF16)


def _compress_prompt(k, v, pwk, pwv, link, linv, nb, seq):
    nseg = seq // CMP_STRIDE
    row = pl.BlockSpec((NSA_KV_GROUPS * seq, NSA_DH), lambda b: (b, 0))
    pw = pl.BlockSpec((NSA_KV_GROUPS * CMP_BLOCK, NSA_DH), lambda b: (0, 0))
    lin = pl.BlockSpec((NSA_DH, NSA_DH), lambda b: (0, 0))
    out = pl.BlockSpec((None, NSA_KV_GROUPS, nseg, NSA_DH), lambda b: (b, 0, 0, 0))
    shp = jax.ShapeDtypeStruct((nb, NSA_KV_GROUPS, nseg, NSA_DH), BF16)
    return pl.pallas_call(
        functools.partial(_compress_prompt_kernel, nseg=nseg),
        grid=(nb,),
        in_specs=[row, row, pw, pw, lin, lin],
        out_specs=[out, out],
        out_shape=[shp, shp],
        compiler_params=_params("parallel"),
        name="compress_prompt",
    )(k.reshape(-1, NSA_DH), v.reshape(-1, NSA_DH), pwk, pwv, link, linv)


def _masked_softmax(s, mask):
    s = jnp.where(mask, s, NEG_INF)
    e = jnp.where(mask, jnp.exp2(s - jnp.max(s, axis=-1, keepdims=True)), 0.0)
    return e / jnp.maximum(jnp.sum(e, axis=-1, keepdims=True), TINY)


def _forced_importance(imp, blk, cur, ns):
    valid = blk <= cur
    forced = valid & ((blk == 0) | (blk > cur - SEL_LOCAL))
    imp = jnp.where(forced, BIG, jnp.where(valid, imp, -BIG))
    return jnp.where(blk < ns, imp, -2.0 * BIG)


def _nsa_prompt_kernel(qn_ref, kcmp_ref, vcmp_ref, ks_ref, vs_ref, kw_ref, vw_ref, zg_ref, o_ref,
                       m_ref, l_ref, acc_ref, *, tq, seq, kc, ntop):
    g = pl.program_id(1)
    q0 = pl.program_id(2) * tq
    jj = NSA_HPG
    nseg = seq // CMP_STRIDE
    ns = seq // SEL_BLOCK

    q = qn_ref[...].reshape(jj * tq, NSA_DH)
    t2 = q0 + lax.broadcasted_iota(jnp.int32, (tq, nseg), 0)
    cmp_last = lax.broadcasted_iota(jnp.int32, (tq, nseg), 1) * CMP_STRIDE + (CMP_BLOCK - 1)
    pc = _masked_softmax(_dot_nt(q, kcmp_ref[...]).reshape(jj, tq, nseg), cmp_last <= t2)
    o_c = _dot(pc.reshape(jj * tq, nseg).astype(BF16), vcmp_ref[...]).reshape(jj, tq, NSA_DH)
    psum = jnp.sum(pc, axis=0)

    blk_r = lax.broadcasted_iota(jnp.int32, (ns, nseg), 0) * SEL_BLOCK
    c_st = lax.broadcasted_iota(jnp.int32, (ns, nseg), 1) * CMP_STRIDE
    ov_t = jnp.where((c_st < blk_r + SEL_BLOCK) & (c_st + CMP_BLOCK > blk_r), 1.0, 0.0).astype(BF16)
    imp_t = _split3_dot_nt(ov_t, psum)
    blk = lax.broadcasted_iota(jnp.int32, (ns, tq), 0)
    cur = (q0 + lax.broadcasted_iota(jnp.int32, (ns, tq), 1)) >> SEL_SHIFT
    imp_t = _forced_importance(imp_t, blk, cur, ns)
    rank = jnp.zeros((ns, tq), F32)
    for mm in range(ns):
        row = imp_t[mm:mm + 1, :]
        tie = jnp.where(blk > mm, 1.0, 0.0)
        rank = rank + jnp.where(row > imp_t, 1.0, jnp.where(row == imp_t, tie, 0.0))
    sel_t = jnp.where(rank < ntop, 1.0, 0.0)
    sel = jnp.concatenate([sel_t, jnp.zeros((LANE - ns, tq), F32)], axis=0).T.astype(BF16)

    m_ref[...] = jnp.full(m_ref.shape, NEG_INF, F32)
    l_ref[...] = jnp.zeros_like(l_ref)
    acc_ref[...] = jnp.zeros_like(acc_ref)
    tk = q0 + lax.broadcasted_iota(jnp.int32, (tq, kc), 0)
    koff = lax.broadcasted_iota(jnp.int32, (tq, kc), 1)
    e_row = lax.broadcasted_iota(jnp.int32, (LANE, kc), 0)
    e_col = lax.broadcasted_iota(jnp.int32, (LANE, kc), 1) >> SEL_SHIFT

    def body(c, carry):
        k0 = pl.multiple_of(c * kc, kc)
        kk = ks_ref[pl.ds(k0, kc), :]
        vv = vs_ref[pl.ds(k0, kc), :]
        expand = jnp.where(e_row == e_col + c * (kc // SEL_BLOCK), 1.0, 0.0).astype(BF16)
        visible = (_dot(sel, expand) > 0.5) & (koff + k0 <= tk)
        hg = NSA_HEAD_GROUP
        for j0 in range(0, jj, hg):
            js = slice(j0, j0 + hg)
            s = _dot_nt(qn_ref[js].reshape(hg * tq, NSA_DH), kk).reshape(hg, tq, kc)
            s = jnp.where(visible, s, NEG_INF)
            m_prev = m_ref[js]
            m_new = jnp.maximum(m_prev, jnp.max(s, axis=-1, keepdims=True))
            alpha = jnp.exp2(m_prev - m_new)
            p = jnp.exp2(s - m_new)
            l_ref[js] = alpha * l_ref[js] + jnp.sum(p, axis=-1, keepdims=True)
            pv = _dot(p.reshape(hg * tq, kc).astype(BF16), vv)
            acc_ref[js] = alpha * acc_ref[js] + pv.reshape(hg, tq, NSA_DH)
            m_ref[js] = m_new
        return carry

    lax.fori_loop(0, (q0 + tq + kc - 1) // kc, body, 0)

    wl = min(WINDOW + tq, seq)
    w0 = pl.multiple_of(jnp.clip(q0 - WINDOW, 0, seq - wl), tq)
    kw = kw_ref[pl.ds(w0, wl), :]
    vw = vw_ref[pl.ds(w0, wl), :]
    tw = q0 + lax.broadcasted_iota(jnp.int32, (tq, wl), 0)
    wpos = w0 + lax.broadcasted_iota(jnp.int32, (tq, wl), 1)
    win_ok = (wpos >= tw - WINDOW) & (wpos <= tw)

    sw = jnp.where(win_ok, _dot_nt(q, kw).reshape(jj, tq, wl), NEG_INF)
    pw = jnp.exp2(sw - jnp.max(sw, axis=-1, keepdims=True))
    o_w = (_dot(pw.reshape(jj * tq, wl).astype(BF16), vw).reshape(jj, tq, NSA_DH)
           / jnp.sum(pw, axis=-1, keepdims=True))
    o_s = acc_ref[...] / l_ref[...]

    zg = zg_ref[...]
    lane = lax.broadcasted_iota(jnp.int32, zg.shape, 1)
    for j in range(jj):
        def gate(k, j=j):
            col = jnp.sum(jnp.where(lane == COL_G % LANE + (g * jj + j) * 3 + k, zg, 0.0), axis=-1, keepdims=True)
            return 1.0 / (1.0 + jnp.exp(-col))
        o = gate(0) * o_c[j] + gate(1) * o_s[j] + gate(2) * o_w[j]
        o_ref[:, j * NSA_DH:(j + 1) * NSA_DH] = o.astype(o_ref.dtype)


def _nsa_prompt(qn, kcmp, vcmp, ks, vs, kw, vw, z, nb, seq):
    tq = min(QUERY_BLOCK, seq)
    kc = min(NSA_KEY_CHUNK, seq)
    nq = seq // tq
    jj = NSA_HPG
    nseg = seq // CMP_STRIDE
    ntop = min(SEL_TOPK, seq // SEL_BLOCK)
    rows = pl.BlockSpec((seq, NSA_DH), lambda b, g, i: (b, g))
    cmp = pl.BlockSpec((None, None, nseg, NSA_DH), lambda b, g, i: (b, g, 0, 0))
    return pl.pallas_call(
        functools.partial(_nsa_prompt_kernel, tq=tq, seq=seq, kc=kc, ntop=ntop),
        grid=(nb, NSA_KV_GROUPS, nq),
        in_specs=[
            pl.BlockSpec((jj, tq, NSA_DH), lambda b, g, i: (g, b * nq + i, 0)),
            cmp, cmp, rows, rows, rows, rows,
            pl.BlockSpec((tq, LANE), lambda b, g, i: (b * nq + i, COL_G // LANE)),
        ],
        out_specs=pl.BlockSpec((tq, jj * NSA_DH), lambda b, g, i: (b * nq + i, g)),
        out_shape=jax.ShapeDtypeStruct((nb * seq, NSA_HEADS * NSA_DH), BF16),
        scratch_shapes=[
            pltpu.VMEM((jj, tq, 1), F32),
            pltpu.VMEM((jj, tq, 1), F32),
            pltpu.VMEM((jj, tq, NSA_DH), F32),
        ],
        compiler_params=_params("parallel", "parallel", "arbitrary"),
        name="nsa_prompt",
    )(qn, kcmp, vcmp, ks, vs, kw, vw, z)


def _sample_select_kernel(pt_ref, *refs, pps, nch, past, nq, ns, ntop):
    del pt_ref
    kp = refs[:pps]
    vp = refs[pps:2 * pps]
    (qn_ref, pwk_ref, pwv_ref, link_ref, linv_ref, ov_ref, oc_ref, sel_ref,
     a0k_ref, a1k_ref, a0v_ref, a1v_ref) = refs[2 * pps:]
    c = pl.program_id(1)
    segs = SEGS_PER_PAGE
    nseg = nch * pps * segs
    for p in range(pps):
        r0 = pl.multiple_of(c * (pps * segs) + p * segs, segs)
        for pg, pw_ref, a0_ref, a1_ref in ((kp[p], pwk_ref, a0k_ref, a1k_ref), (vp[p], pwv_ref, a0v_ref, a1v_ref)):
            part = _page_segment_sums(pg, 0, pw_ref)
            for g in range(NSA_KV_GROUPS):
                a0_ref[g, pl.ds(r0, segs), :] = part[0][g]
                a1_ref[g, pl.ds(r0, segs), :] = part[1][g]

    @pl.when(c == nch - 1)
    def _():
        qp = QUERY_PAD
        rows = NSA_HPG * qp
        qi = lax.broadcasted_iota(jnp.int32, (1, qp, nseg), 1)
        t3 = past + jnp.minimum(qi, nq - 1)
        cmp_last = lax.broadcasted_iota(jnp.int32, (1, qp, nseg), 2) * CMP_STRIDE + (CMP_BLOCK - 1)
        imps = []
        for g in range(NSA_KV_GROUPS):
            acc_k = (a0k_ref[g] + pltpu.roll(a1k_ref[g], nseg - 1, 0)).astype(BF16)
            acc_v = (a0v_ref[g] + pltpu.roll(a1v_ref[g], nseg - 1, 0)).astype(BF16)
            kc = _dot(acc_k, link_ref[...]).astype(BF16)
            vc = _dot(acc_v, linv_ref[...]).astype(BF16)
            sc = _dot_nt(qn_ref[g], kc).reshape(NSA_HPG, qp, nseg)
            pc = _masked_softmax(sc, cmp_last <= t3)
            oc_ref[g] = _dot(pc.reshape(rows, nseg).astype(BF16), vc)
            imps.append(_split3_dot(jnp.sum(pc, axis=0), ov_ref[...]))
        imp = jnp.concatenate(imps, axis=0)
        selw = imp.shape[1]
        blk = lax.broadcasted_iota(jnp.int32, imp.shape, 1)
        qrow = lax.broadcasted_iota(jnp.int32, imp.shape, 0) & (qp - 1)
        cur = (past + jnp.minimum(qrow, nq - 1)) >> SEL_SHIFT
        work = _forced_importance(imp, blk, cur, ns)
        blk_f = blk.astype(F32)
        chosen = jnp.zeros(imp.shape, F32)
        for _ in range(ntop):
            top = jnp.max(work, axis=-1, keepdims=True)
            first = jnp.min(jnp.where(work == top, blk_f, float(selw)), axis=-1, keepdims=True)
            hit = blk_f == first
            chosen = jnp.where(hit, 1.0, chosen)
            work = jnp.where(hit, -4.0 * BIG, work)
        sel_ref[...] = chosen


def _sample_select(page_table, kpool, vpool, qn, pwk, pwv, link, linv, ov, past, nq, ns, ntop):
    db, npages = page_table.shape
    pps = min(PAGES_PER_STEP, npages)
    nch = npages // pps
    nseg = npages * (PAGE_ROWS // CMP_STRIDE)
    rows = NSA_HPG * QUERY_PAD
    selw = ov.shape[1]
    pages = [pl.BlockSpec((None, NSA_KV_GROUPS * PAGE_ROWS, NSA_DH),
                          lambda b, c, pt, p=p: (pt[b * npages + c * pps + p], 0, 0)) for p in range(pps)]
    const2 = lambda b, c, pt: (0, 0)
    grid_spec = pltpu.PrefetchScalarGridSpec(
        num_scalar_prefetch=1,
        grid=(db, nch),
        in_specs=pages + pages + [
            pl.BlockSpec((None, NSA_KV_GROUPS, rows, NSA_DH), lambda b, c, pt: (b, 0, 0, 0)),
            pl.BlockSpec((NSA_KV_GROUPS * CMP_BLOCK, NSA_DH), const2),
            pl.BlockSpec((NSA_KV_GROUPS * CMP_BLOCK, NSA_DH), const2),
            pl.BlockSpec((NSA_DH, NSA_DH), const2),
            pl.BlockSpec((NSA_DH, NSA_DH), const2),
            pl.BlockSpec(ov.shape, const2),
        ],
        out_specs=[
            pl.BlockSpec((None, NSA_KV_GROUPS, rows, NSA_DH), lambda b, c, pt: (b, 0, 0, 0)),
            pl.BlockSpec((None, NSA_KV_GROUPS * QUERY_PAD, selw), lambda b, c, pt: (b, 0, 0)),
        ],
        scratch_shapes=[pltpu.VMEM((NSA_KV_GROUPS, nseg, NSA_DH), F32) for _ in range(4)],
    )
    return pl.pallas_call(
        functools.partial(_sample_select_kernel, pps=pps, nch=nch, past=past, nq=nq, ns=ns, ntop=ntop),
        grid_spec=grid_spec,
        out_shape=[
            jax.ShapeDtypeStruct((db, NSA_KV_GROUPS, rows, NSA_DH), F32),
            jax.ShapeDtypeStruct((db, NSA_KV_GROUPS * QUERY_PAD, selw), F32),
        ],
        compiler_params=_params("parallel", "arbitrary"),
        name="sample_select",
    )(page_table.reshape(-1), *([kpool] * pps), *([vpool] * pps), qn, pwk, pwv, link, linv, ov)


def _sample_attend_kernel(pt_ref, pt_sel_ref, *refs, pps, nch, past, nq, win_buf):
    del pt_ref, pt_sel_ref
    ckv_p = refs[:pps]
    kr_p = refs[pps:2 * pps]
    ks_p = refs[2 * pps:3 * pps]
    vs_p = refs[3 * pps:4 * pps]
    (ql_ref, qr_ref, qn_ref, chosen_ref, sell_ref, oc_ref, gt_ref, ckvn_ref, krn_ref, ksn_ref, vsn_ref,
     kw_ref, vw_ref, wuv_ref, omla_ref, onsa_ref,
     kc_s, kr_s, ks_s, vs_s, m1_ref, l1_ref, acc1_ref, m2_ref, l2_ref, acc2_ref) = refs[4 * pps:]
    c = pl.program_id(1)
    qp = QUERY_PAD
    jj = NSA_HPG
    hh = MLA_HEADS
    kk = pps * PAGE_ROWS

    @pl.when(c == 0)
    def _():
        m1_ref[...] = jnp.full(m1_ref.shape, NEG_INF, F32)
        l1_ref[...] = jnp.zeros_like(l1_ref)
        acc1_ref[...] = jnp.zeros_like(acc1_ref)
        m2_ref[...] = jnp.full(m2_ref.shape, NEG_INF, F32)
        l2_ref[...] = jnp.zeros_like(l2_ref)
        acc2_ref[...] = jnp.zeros_like(acc2_ref)

    for p in range(pps):
        sl = slice(p * PAGE_ROWS, (p + 1) * PAGE_ROWS)
        sl2 = slice(p * PAGE_IROWS, (p + 1) * PAGE_IROWS)
        kc_s[sl, :] = ckv_p[p][...].astype(BF16)
        kr_s[:, sl] = kr_p[p][...].astype(BF16)
        ks_s[sl2, :] = ks_p[p][...].astype(BF16)
        vs_s[sl2, :] = vs_p[p][...].astype(BF16)

    def online(m_ref, l_ref, acc_ref, s, v):
        m_prev = m_ref[...]
        m_new = jnp.maximum(m_prev, jnp.max(s, axis=-1, keepdims=True))
        alpha = jnp.exp2(m_prev - m_new)
        p = jnp.exp2(s - m_new)
        l_ref[...] = alpha * l_ref[...] + jnp.sum(p, axis=-1, keepdims=True)
        acc_ref[...] = alpha * acc_ref[...] + _dot(p.astype(BF16), v)
        m_ref[...] = m_new

    ql = ql_ref[...]
    qr = qr_ref[...]
    kc = kc_s[...]
    s1 = _dot_nt(ql, kc) + _dot(qr, kr_s[...])
    online(m1_ref, l1_ref, acc1_ref, s1, kc)

    gg = NSA_KV_GROUPS
    qn = qn_ref[...]
    chosen = chosen_ref[...].astype(F32)
    s2 = _dot_nt(qn, ks_s[...]).reshape(gg, jj, qp, gg * kk)
    s2 = jnp.where(chosen.reshape(gg, 1, qp, gg * kk) > 0.5, s2, NEG_INF).reshape(gg * jj * qp, gg * kk)
    online(m2_ref, l2_ref, acc2_ref, s2, vs_s[...])

    @pl.when(c == nch - 1)
    def _():
        npad = NEW_PAD
        kidx = lax.broadcasted_iota(jnp.int32, (1, qp, npad), 2)
        qidx = lax.broadcasted_iota(jnp.int32, (1, qp, npad), 1)
        new_ok = (kidx <= qidx) & (kidx < nq)
        ckvn = ckvn_ref[...]
        s1n = _dot_nt(ql, ckvn) + _dot_nt(qr, krn_ref[...])
        s1n = jnp.where(new_ok, s1n.reshape(hh, qp, npad), NEG_INF).reshape(hh * qp, npad)
        online(m1_ref, l1_ref, acc1_ref, s1n, ckvn)
        o_lat = (acc1_ref[...] / l1_ref[...]).astype(BF16)
        for h in range(hh):
            omla_ref[:, h * MLA_V:(h + 1) * MLA_V] = _dot(o_lat[h * qp:(h + 1) * qp], wuv_ref[h]).astype(omla_ref.dtype)

        wl = kw_ref.shape[0]
        widx = lax.broadcasted_iota(jnp.int32, (1, qp, wl), 2)
        wpos = past - win_buf + widx
        tw = past + jnp.minimum(lax.broadcasted_iota(jnp.int32, (1, qp, wl), 1), nq - 1)
        win_ok = (widx < win_buf + nq) & (wpos >= tw - WINDOW) & (wpos <= tw)
        shp = (gg, 1, qp, gg * npad)
        ncol = lax.broadcasted_iota(jnp.int32, shp, 3)
        nrow_g = lax.broadcasted_iota(jnp.int32, shp, 0)
        nq_i = lax.broadcasted_iota(jnp.int32, shp, 2)
        nkk = ncol & (npad - 1)
        sel_new = (((ncol >> NEW_SHIFT) == nrow_g) & (nkk <= nq_i) & (nkk < nq)
                   & (sell_ref[:, 0:gg * npad].reshape(shp) > 0.5))
        s2n = _dot_nt(qn, ksn_ref[...]).reshape(gg, jj, qp, gg * npad)
        s2n = jnp.where(sel_new, s2n, NEG_INF).reshape(gg * jj * qp, gg * npad)
        online(m2_ref, l2_ref, acc2_ref, s2n, vsn_ref[...])
        o_s_all = acc2_ref[...] / l2_ref[...]
        for g in range(NSA_KV_GROUPS):
            cols = slice(g * NSA_DH, (g + 1) * NSA_DH)
            rws = slice(g * jj * qp, (g + 1) * jj * qp)
            qg = qn[rws]
            o_s = o_s_all[rws]

            sw = _dot_nt(qg, kw_ref[:, cols]).reshape(jj, qp, wl)
            sw = jnp.where(win_ok, sw, NEG_INF).reshape(jj * qp, wl)
            pw = jnp.exp2(sw - jnp.max(sw, axis=-1, keepdims=True))
            pw = pw / jnp.sum(pw, axis=-1, keepdims=True)
            o_w = _dot(pw.astype(BF16), vw_ref[:, cols])

            gt = gt_ref[rws, :]
            o = gt[:, 0:1] * oc_ref[rws, :] + gt[:, 1:2] * o_s + gt[:, 2:3] * o_w
            for j in range(jj):
                hcol = (g * jj + j) * NSA_DH
                onsa_ref[:, hcol:hcol + NSA_DH] = o[j * qp:(j + 1) * qp].astype(onsa_ref.dtype)


def _sample_attend(page_table, page_table_sel, ckv_pool, kr_pool, ks_pool, vs_pool, ql, qr, qn, chosen, sell, oc, gt,
                   ckvn, krn, ksn, vsn, kw, vw, wuv, past, nq, win_buf):
    db, npages = page_table.shape
    pps = min(PAGES_PER_STEP, npages)
    nch = npages // pps
    kk = pps * PAGE_ROWS
    qp = QUERY_PAD
    hh = MLA_HEADS
    rows = NSA_HPG * qp

    def pages(rows_, width, chosen_only=False):
        def index_map(b, c, pt, pt_sel, p=0):
            return ((pt_sel if chosen_only else pt)[b * npages + c * pps + p], 0, 0)
        return [pl.BlockSpec((None, rows_, width), functools.partial(index_map, p=p)) for p in range(pps)]

    def per_seq(shape):
        nd = len(shape)
        return pl.BlockSpec((None,) + tuple(shape), lambda b, c, pt, pt_sel: (b,) + (0,) * nd)

    def const(shape):
        nd = len(shape)
        return pl.BlockSpec(tuple(shape), lambda b, c, pt, pt_sel: (0,) * nd)

    in_specs = (
        pages(PAGE_ROWS, MLA_KV_RANK) + pages(MLA_ROPE, PAGE_ROWS)
        + pages(PAGE_IROWS, NSA_DH, True) + pages(PAGE_IROWS, NSA_DH, True) + [
            per_seq((hh * qp, MLA_KV_RANK)),
            per_seq((hh * qp, MLA_ROPE)),
            per_seq((NSA_KV_GROUPS * rows, NSA_DH)),
            pl.BlockSpec((None, None, NSA_KV_GROUPS * qp, NSA_KV_GROUPS * kk), lambda b, c, pt, pt_sel: (b, c, 0, 0)),
            per_seq((NSA_KV_GROUPS * qp, LANE)),
            per_seq((NSA_KV_GROUPS * rows, NSA_DH)),
            per_seq((NSA_KV_GROUPS * rows, 3)),
            per_seq((NEW_PAD, MLA_KV_RANK)),
            per_seq((NEW_PAD, MLA_ROPE)),
            per_seq((NSA_KV_GROUPS * NEW_PAD, NSA_DH)),
            per_seq((NSA_KV_GROUPS * NEW_PAD, NSA_DH)),
            per_seq(kw.shape[1:]),
            per_seq(vw.shape[1:]),
            const(wuv.shape),
        ])
    grid_spec = pltpu.PrefetchScalarGridSpec(
        num_scalar_prefetch=2,
        grid=(db, nch),
        in_specs=in_specs,
        out_specs=[per_seq((qp, hh * MLA_V)), per_seq((qp, NSA_HEADS * NSA_DH))],
        scratch_shapes=[
            pltpu.VMEM((kk, MLA_KV_RANK), BF16),
            pltpu.VMEM((MLA_ROPE, kk), BF16),
            pltpu.VMEM((NSA_KV_GROUPS * kk, NSA_DH), BF16),
            pltpu.VMEM((NSA_KV_GROUPS * kk, NSA_DH), BF16),
            pltpu.VMEM((hh * qp, 1), F32),
            pltpu.VMEM((hh * qp, 1), F32),
            pltpu.VMEM((hh * qp, MLA_KV_RANK), F32),
            pltpu.VMEM((NSA_KV_GROUPS * rows, 1), F32),
            pltpu.VMEM((NSA_KV_GROUPS * rows, 1), F32),
            pltpu.VMEM((NSA_KV_GROUPS * rows, NSA_DH), F32),
        ],
    )
    return pl.pallas_call(
        functools.partial(_sample_attend_kernel, pps=pps, nch=nch, past=past, nq=nq, win_buf=win_buf),
        grid_spec=grid_spec,
        out_shape=[
            jax.ShapeDtypeStruct((db, qp, hh * MLA_V), BF16),
            jax.ShapeDtypeStruct((db, qp, NSA_HEADS * NSA_DH), BF16),
        ],
        compiler_params=_params("parallel", "arbitrary"),
        name="sample_attend",
    )(page_table.reshape(-1), page_table_sel.reshape(-1),
      *([ckv_pool] * pps), *([kr_pool] * pps), *([ks_pool] * pps), *([vs_pool] * pps),
      ql, qr, qn, chosen, sell, oc, gt, ckvn, krn, ksn, vsn, kw, vw, wuv)


def _outproj_kernel(x_ref, a_ref, b_ref, wa_ref, wb_ref, o_ref):
    o_ref[...] = x_ref[...] + _dot(a_ref[...], wa_ref[...]) + _dot(b_ref[...], wb_ref[...])


def _outproj(x, a, b, wa, wb):
    n, d = x.shape
    tm = _row_tile(n, 512)
    return pl.pallas_call(
        _outproj_kernel,
        grid=(n // tm,),
        in_specs=[
            pl.BlockSpec((tm, d), lambda i: (i, 0)),
            pl.BlockSpec((tm, a.shape[1]), lambda i: (i, 0)),
            pl.BlockSpec((tm, b.shape[1]), lambda i: (i, 0)),
            pl.BlockSpec(wa.shape, lambda i: (0, 0)),
            pl.BlockSpec(wb.shape, lambda i: (0, 0)),
        ],
        out_specs=pl.BlockSpec((tm, d), lambda i: (i, 0)),
        out_shape=jax.ShapeDtypeStruct((n, d), F32),
        compiler_params=_params("parallel"),
        name="outproj",
    )(x, a, b, wa, wb)


def _rope_tables(pos, rot_dim):
    inv = ROPE_THETA ** (-jnp.arange(0, rot_dim, 2, dtype=F32) / rot_dim)
    ang = pos.astype(F32)[:, None] * inv[None, :]
    return jnp.cos(ang), jnp.sin(ang)


def _rope(x, cos, sin):
    half = cos.shape[-1]
    c = cos[:, None, :]
    s = sin[:, None, :]
    x1, x2 = x[..., :half], x[..., half:2 * half]
    return jnp.concatenate([x1 * c - x2 * s, x2 * c + x1 * s, x[..., 2 * half:]], axis=-1)


def _pad_queries(a, db, nq):
    heads, d = a.shape[1:]
    a = a.reshape(db, nq, heads, d).transpose(0, 2, 1, 3)
    a = jnp.pad(a, ((0, 0), (0, 0), (0, QUERY_PAD - nq), (0, 0)))
    return a.reshape(db, heads * QUERY_PAD, d)


def _pad_new(a, db, nq):
    a = a.reshape(db, nq, a.shape[-1])
    return jnp.pad(a, ((0, 0), (0, NEW_PAD - nq), (0, 0))).astype(BF16)


def _pad_new_groups(a, db, nq):
    a = a.reshape(db, nq, NSA_KV_GROUPS, NSA_DH).transpose(0, 2, 1, 3)
    a = jnp.pad(a, ((0, 0), (0, 0), (0, NEW_PAD - nq), (0, 0)))
    return a.reshape(db, NSA_KV_GROUPS * NEW_PAD, NSA_DH).astype(BF16)


def kernel(x_prompt, x_sample, cache_mla_ckv, cache_mla_krope, cache_nsa_k_cmp, cache_nsa_v_cmp, cache_nsa_k_sel, cache_nsa_v_sel, state_nsa_k_win, state_nsa_v_win, page_table, ffn1_norm, w_ffn1_gate, w_ffn1_up, w_ffn1_down, mix_norm, w_in, mla_q_norm, w_mla_q_up, mla_kv_norm, w_mla_k_up, w_mla_v_up, nsa_cmp_pos_k, nsa_cmp_lin_k, nsa_cmp_pos_v, nsa_cmp_lin_v, w_out, ffn2_norm, w_ffn2_gate, w_ffn2_up, w_ffn2_down, final_norm):
    nb, seq, d = x_prompt.shape
    db, nq = x_sample.shape[:2]
    depth, n_pool = cache_mla_ckv.shape[:2]
    npages = page_table.shape[1]
    past = npages * PAGE_ROWS
    win_buf = state_nsa_k_win.shape[2]
    np_tok = nb * seq
    ns_tok = db * nq
    assert nq <= QUERY_PAD and cache_mla_ckv.shape[2] == PAGE_ROWS and seq % QUERY_BLOCK == 0
    hh, gg, dh = MLA_HEADS, NSA_KV_GROUPS, NSA_DH
    x = jnp.concatenate([x_prompt.reshape(np_tok, d), x_sample.reshape(ns_tok, d)], axis=0)
    pos = jnp.concatenate([jnp.tile(jnp.arange(seq), nb), jnp.tile(past + jnp.arange(nq), db)])
    cos_m, sin_m = _rope_tables(pos, MLA_ROPE)
    cos_n, sin_n = _rope_tables(pos, NSA_ROT)

    ns_s = -(-(past + nq) // SEL_BLOCK)
    selw = -(-ns_s // LANE) * LANE
    nseg_s = past // CMP_STRIDE
    c_st = np.arange(nseg_s)[:, None] * CMP_STRIDE
    s_st = np.arange(selw)[None, :] * SEL_BLOCK
    overlap = jnp.asarray(((c_st < s_st + SEL_BLOCK) & (c_st + CMP_BLOCK > s_st)), dtype=BF16)
    pps = min(PAGES_PER_STEP, npages)
    nch = npages // pps
    bpc = pps * PAGE_ROWS // SEL_BLOCK
    own_group = jnp.asarray(np.arange(gg * QUERY_PAD)[:, None] // QUERY_PAD
                            == np.arange(gg * pps * PAGE_ROWS)[None, :] % gg, dtype=BF16)

    rows_p = [[] for _ in range(8)]
    rows_s = [[] for _ in range(8)]
    for l in range(depth):
        wq = w_mla_q_up[l].reshape(MLA_Q_RANK, hh, MLA_NOPE + MLA_ROPE)
        wqn = wq[:, :, :MLA_NOPE].reshape(MLA_Q_RANK, hh * MLA_NOPE).astype(BF16)
        wqr = wq[:, :, MLA_NOPE:].reshape(MLA_Q_RANK, hh * MLA_ROPE).astype(BF16)
        wuk = w_mla_k_up[l].reshape(MLA_KV_RANK, hh, MLA_NOPE).transpose(1, 2, 0).astype(BF16)
        wuv = w_mla_v_up[l].reshape(MLA_KV_RANK, hh, MLA_V).transpose(1, 0, 2).astype(BF16)
        wi = w_in[l]
        o_q, o_kv, o_kr = 0, MLA_Q_RANK, MLA_Q_RANK + MLA_KV_RANK
        o_qn = o_kr + MLA_ROPE
        o_kv6 = o_qn + NSA_HEADS * dh
        o_g = o_kv6 + 6 * KV_COLS
        w_in_p = jnp.concatenate([
            wi[:, o_q:o_kr], wi[:, o_qn:o_g], wi[:, o_kr:o_qn], wi[:, o_g:],
            jnp.zeros((d, IN_PAD - wi.shape[1]), wi.dtype)], axis=1).astype(BF16)
        pwk, pwv = (jnp.repeat(w, NSA_KV_GROUPS, axis=0) for w in (nsa_cmp_pos_k[l], nsa_cmp_pos_v[l]))
        link = nsa_cmp_lin_k[l].astype(BF16)
        linv = nsa_cmp_lin_v[l].astype(BF16)

        x = _ffn(x, ffn1_norm[l], w_ffn1_gate[l].astype(BF16), w_ffn1_up[l].astype(BF16),
                 w_ffn1_down[l].astype(BF16))
        z = _inproj(x, mix_norm[l], w_in_p)
        ql, qr_raw, ckv = _mlaprep(z, mla_q_norm[l], mla_kv_norm[l], wqn, wqr, wuk)

        qr = _rope(qr_raw.reshape(-1, hh, MLA_ROPE), cos_m, sin_m) * MLA_QSCALE
        krope = _rope(z[:, None, COL_KR:COL_KR + MLA_ROPE], cos_m, sin_m)[:, 0]
        qn = _rope(z[:, COL_QN:COL_KC].reshape(-1, NSA_HEADS, dh), cos_n, sin_n) * NSA_QSCALE

        def kv_rot(col):
            return _rope(z[:, col:col + KV_COLS].reshape(-1, gg, dh), cos_n, sin_n).reshape(-1, KV_COLS)

        k_cmp, k_sel, k_win = kv_rot(COL_KC), kv_rot(COL_KS), kv_rot(COL_KW)
        v_cmp, v_sel, v_win = (z[:, c0:c0 + KV_COLS] for c0 in (COL_VC, COL_VS, COL_VW))
        ckv_b, krope_b = ckv.astype(BF16), krope.astype(BF16)
        k_sel_b, v_sel_b, k_win_b, v_win_b = (a.astype(BF16) for a in (k_sel, v_sel, k_win, v_win))

        qr_t = qr.transpose(1, 0, 2).astype(BF16)
        qn_t = qn.transpose(1, 0, 2).astype(BF16)
        o_mla_p = _mla_prompt(ql, qr_t, ckv_b, krope_b, wuv, nb, seq)
        kcmp_p, vcmp_p = _compress_prompt(k_cmp, v_cmp, pwk, pwv, link, linv, nb, seq)
        o_nsa_p = _nsa_prompt(qn_t, kcmp_p, vcmp_p, k_sel_b, v_sel_b, k_win_b, v_win_b, z, nb, seq)

        sm = slice(np_tok, None)
        ql_s = _pad_queries(ql[:, sm].transpose(1, 0, 2), db, nq)
        qr_s = _pad_queries(qr[sm], db, nq).astype(BF16)
        qn_s = _pad_queries(qn[sm], db, nq).astype(BF16)
        gates = jax.nn.sigmoid(z[sm, COL_G:COL_G + GATE_COLS]).reshape(ns_tok, NSA_HEADS, 3)
        gt_s = _pad_queries(gates, db, nq)
        ckv_pool = cache_mla_ckv[l]
        kr_pool = jnp.swapaxes(cache_mla_krope[l], 1, 2)
        kc_pool, vc_pool, ks_pool, vs_pool = (c[l].reshape(n_pool, PAGE_IROWS, dh) for c in (
            cache_nsa_k_cmp, cache_nsa_v_cmp, cache_nsa_k_sel, cache_nsa_v_sel))
        o_c, sel = _sample_select(page_table, kc_pool, vc_pool, qn_s.reshape(db, gg, NSA_HPG * QUERY_PAD, dh),
                                  pwk, pwv, link, linv, overlap, past, nq, ns_s, min(SEL_TOPK, ns_s))
        sel_blocks = sel[:, :, :nch * bpc].reshape(db, gg * QUERY_PAD, nch, bpc).transpose(0, 2, 1, 3)
        chosen = jnp.repeat(sel_blocks.astype(BF16), gg * SEL_BLOCK, axis=-1) * own_group
        real = sel.reshape(db, gg, QUERY_PAD, -1)[:, :, :nq, :nch * bpc]
        page_needed = jnp.max(real.reshape(db, gg * nq, npages, PAGE_ROWS // SEL_BLOCK), axis=(1, 3)) > 0.5
        step = jnp.arange(db * nch, dtype=jnp.int32)[:, None]
        last = lax.cummax(jnp.where(page_needed.reshape(db * nch, pps), step, 0), axis=0)
        page_table_sel = jnp.take_along_axis(page_table.reshape(db * nch, pps), last, axis=0).reshape(db, npages)
        sell = jnp.broadcast_to(sel[:, :, nch * bpc:nch * bpc + 1], (db, gg * QUERY_PAD, LANE))
        kwin_full = jnp.concatenate([state_nsa_k_win[l].reshape(db, win_buf, KV_COLS),
                                     k_win[sm].reshape(db, nq, KV_COLS)], axis=1)
        vwin_full = jnp.concatenate([state_nsa_v_win[l].reshape(db, win_buf, KV_COLS),
                                     v_win[sm].reshape(db, nq, KV_COLS)], axis=1)
        wpad = ((0, 0), (0, NEW_PAD - nq), (0, 0))
        o_mla_s, o_nsa_s = _sample_attend(
            page_table, page_table_sel, ckv_pool, kr_pool, ks_pool, vs_pool, ql_s, qr_s, qn_s, chosen, sell,
            o_c.reshape(db, -1, dh), gt_s,
            _pad_new(ckv[sm], db, nq), _pad_new(krope[sm], db, nq), _pad_new_groups(k_sel[sm], db, nq),
            _pad_new_groups(v_sel[sm], db, nq), jnp.pad(kwin_full, wpad).astype(BF16),
            jnp.pad(vwin_full, wpad).astype(BF16), wuv, past, nq, win_buf)

        mix_a = jnp.concatenate([o_mla_p, o_mla_s[:, :nq].reshape(ns_tok, -1)], axis=0)
        mix_b = jnp.concatenate([o_nsa_p, o_nsa_s[:, :nq].reshape(ns_tok, -1)], axis=0)
        wo = w_out[l].astype(BF16)
        x = _outproj(x, mix_a, mix_b, wo[:hh * MLA_V], wo[hh * MLA_V:])
        x = _ffn(x, ffn2_norm[l], w_ffn2_gate[l].astype(BF16), w_ffn2_up[l].astype(BF16),
                 w_ffn2_down[l].astype(BF16), final_g=final_norm if l == depth - 1 else None)

        win_p = min(WINDOW, seq)
        new = (ckv, krope, k_cmp, v_cmp, k_sel, v_sel)
        for i, a in enumerate(new):
            tail = a.shape[1:] if i < 2 else (gg, dh)
            rows_p[i].append(a[:np_tok].reshape((nb, seq) + tail))
            rows_s[i].append(a[np_tok:].reshape((db, nq) + tail))
        rows_p[6].append(k_win[:np_tok].reshape(nb, seq, gg, dh)[:, -win_p:])
        rows_p[7].append(v_win[:np_tok].reshape(nb, seq, gg, dh)[:, -win_p:])
        rows_s[6].append(kwin_full[:, -win_buf:].reshape(db, win_buf, gg, dh))
        rows_s[7].append(vwin_full[:, -win_buf:].reshape(db, win_buf, gg, dh))

    y_prompt = x[:np_tok].reshape(nb, seq, d)
    y_sample = x[np_tok:].reshape(db, nq, d)
    out = [y_prompt, y_sample]
    for i in range(8):
        out += [jnp.stack(rows_p[i]), jnp.stack(rows_s[i])]
    return tuple(out)
```

```python
import functools

import jax
import jax.numpy as jnp
import numpy as np
from jax import lax
from jax.experimental import pallas as pl
from jax.experimental.pallas import tpu as pltpu

MLA_HEADS = 8
MLA_NOPE = 128
MLA_ROPE = 64
MLA_V = 128
MLA_Q_RANK = 512
MLA_KV_RANK = 256
NSA_HEADS = 8
NSA_KV_GROUPS = 2
NSA_HPG = NSA_HEADS // NSA_KV_GROUPS
NSA_DH = 128
CMP_BLOCK = 32
CMP_STRIDE = 16
SEL_BLOCK = 64
SEL_TOPK = 16
SEL_LOCAL = 2
WINDOW = 512
ROPE_THETA = 500000.0
NSA_ROT = NSA_DH // 4
RMS_EPS = 1e-6
QUERY_BLOCK = 128
MLA_SCALE = (MLA_NOPE + MLA_ROPE) ** -0.5
NSA_SCALE = NSA_DH ** -0.5
LOG2E = 1.4426950408889634
MLA_QSCALE = MLA_SCALE * LOG2E
NSA_QSCALE = NSA_SCALE * LOG2E
NEG_INF = -1e30
BIG = 1e6
TINY = 1e-30
KV_COLS = NSA_KV_GROUPS * NSA_DH
GATE_COLS = 3 * NSA_HEADS

COL_Q = 0
COL_KV = COL_Q + MLA_Q_RANK
COL_QN = COL_KV + MLA_KV_RANK
COL_KC = COL_QN + NSA_HEADS * NSA_DH
COL_VC = COL_KC + KV_COLS
COL_KS = COL_VC + KV_COLS
COL_VS = COL_KS + KV_COLS
COL_KW = COL_VS + KV_COLS
COL_VW = COL_KW + KV_COLS
COL_KR = COL_VW + KV_COLS
COL_G = COL_KR + MLA_ROPE
LANE = 128
IN_PAD = -(-(COL_G + GATE_COLS) // (3 * LANE)) * (3 * LANE)

QUERY_PAD = 8
NEW_PAD = 16
PAGE_ROWS = 128
SEL_SHIFT = SEL_BLOCK.bit_length() - 1
NEW_SHIFT = NEW_PAD.bit_length() - 1
PAGES_PER_STEP = 16
MLA_KEY_CHUNK = 1024
MLA_HEAD_GROUP = 4
NSA_KEY_CHUNK = 1024
NSA_HEAD_GROUP = 4
VMEM_LIMIT = 56 * 1024 * 1024

F32 = jnp.float32
BF16 = jnp.bfloat16


def _dot(a, b):
    return jnp.dot(a, b, preferred_element_type=F32)


def _dot_nt(a, b):
    return lax.dot_general(a, b, (((1,), (1,)), ((), ())), preferred_element_type=F32)


def _rms(x, g):
    return x * lax.rsqrt(jnp.mean(x * x, axis=-1, keepdims=True) + RMS_EPS) * g


def _split3_dot_nt(w_bf16, x):
    hi = x.astype(BF16)
    r1 = x - hi.astype(F32)
    mid = r1.astype(BF16)
    lo = (r1 - mid.astype(F32)).astype(BF16)
    return _dot_nt(w_bf16, hi) + _dot_nt(w_bf16, mid) + _dot_nt(w_bf16, lo)


def _split3_dot(x, w_bf16):
    hi = x.astype(BF16)
    r1 = x - hi.astype(F32)
    mid = r1.astype(BF16)
    lo = (r1 - mid.astype(F32)).astype(BF16)
    return _dot(hi, w_bf16) + _dot(mid, w_bf16) + _dot(lo, w_bf16)


def _row_tile(n, want):
    t = min(want, n)
    while n % t:
        t -= 8
    return t


def _params(*sem):
    return pltpu.CompilerParams(dimension_semantics=sem, vmem_limit_bytes=VMEM_LIMIT)


def _ffn_kernel(x_ref, g_ref, wg_ref, wu_ref, wd_ref, *rest, final):
    if final:
        fg_ref, o_ref, h_ref, acc_ref = rest
    else:
        o_ref, h_ref, acc_ref = rest
    j = pl.program_id(1)

    @pl.when(j == 0)
    def _():
        h_ref[...] = _rms(x_ref[...], g_ref[...]).astype(BF16)
        acc_ref[...] = jnp.zeros_like(acc_ref)

    h = h_ref[...]
    a = _dot(h, wg_ref[...])
    u = _dot(h, wu_ref[...])
    act = (a / (1.0 + jnp.exp(-a))) * u
    acc_ref[...] += _dot(act.astype(BF16), wd_ref[...])

    @pl.when(j == pl.num_programs(1) - 1)
    def _():
        y = x_ref[...] + 0.5 * acc_ref[...]
        if final:
            y = _rms(y, fg_ref[...])
        o_ref[...] = y


def _ffn(x, g, wg, wu, wd, final_g=None):
    n, d = x.shape
    dff = wg.shape[1]
    tm = _row_tile(n, 512)
    tf = 512 if dff % 512 == 0 else dff
    in_specs = [
        pl.BlockSpec((tm, d), lambda i, j: (i, 0)),
        pl.BlockSpec((1, d), lambda i, j: (0, 0)),
        pl.BlockSpec((d, tf), lambda i, j: (0, j)),
        pl.BlockSpec((d, tf), lambda i, j: (0, j)),
        pl.BlockSpec((tf, d), lambda i, j: (j, 0)),
    ]
    args = [x, g.reshape(1, d), wg, wu, wd]
    if final_g is not None:
        in_specs.append(pl.BlockSpec((1, d), lambda i, j: (0, 0)))
        args.append(final_g.reshape(1, d))
    return pl.pallas_call(
        functools.partial(_ffn_kernel, final=final_g is not None),
        grid=(n // tm, dff // tf),
        in_specs=in_specs,
        out_specs=pl.BlockSpec((tm, d), lambda i, j: (i, 0)),
        out_shape=jax.ShapeDtypeStruct((n, d), F32),
        scratch_shapes=[pltpu.VMEM((tm, d), BF16), pltpu.VMEM((tm, d), F32)],
        compiler_params=_params("parallel", "arbitrary"),
        name="ffn",
    )(*args)


def _inproj_kernel(x_ref, g_ref, w_ref, o_ref, h_ref):
    @pl.when(pl.program_id(1) == 0)
    def _():
        h_ref[...] = _rms(x_ref[...], g_ref[...]).astype(BF16)

    o_ref[...] = _dot(h_ref[...], w_ref[...])


def _inproj(x, g, w):
    n, d = x.shape
    nout = w.shape[1]
    tm = _row_tile(n, 512)
    tn = nout // 3
    return pl.pallas_call(
        _inproj_kernel,
        grid=(n // tm, nout // tn),
        in_specs=[
            pl.BlockSpec((tm, d), lambda i, j: (i, 0)),
            pl.BlockSpec((1, d), lambda i, j: (0, 0)),
            pl.BlockSpec((d, tn), lambda i, j: (0, j)),
        ],
        out_specs=pl.BlockSpec((tm, tn), lambda i, j: (i, j)),
        out_shape=jax.ShapeDtypeStruct((n, nout), F32),
        scratch_shapes=[pltpu.VMEM((tm, d), BF16)],
        compiler_params=_params("parallel", "arbitrary"),
        name="inproj",
    )(x, g.reshape(1, d), w)


def _mlaprep_kernel(zq_ref, zkv_ref, gq_ref, gkv_ref, wqn_ref, wqr_ref, wuk_ref, ql_ref, qr_ref, ckv_ref):
    cq = _rms(zq_ref[...], gq_ref[...]).astype(BF16)
    qn = _dot(cq, wqn_ref[...]).astype(BF16)
    for h in range(MLA_HEADS):
        ql_ref[h] = (_dot(qn[:, h * MLA_NOPE:(h + 1) * MLA_NOPE], wuk_ref[h]) * MLA_QSCALE).astype(BF16)
    qr_ref[...] = _dot(cq, wqr_ref[...])
    ckv_ref[...] = _rms(zkv_ref[...], gkv_ref[...])


def _mlaprep(z, gq, gkv, wqn, wqr, wuk):
    n = z.shape[0]
    tm = _row_tile(n, 512)
    return pl.pallas_call(
        _mlaprep_kernel,
        grid=(n // tm,),
        in_specs=[
            pl.BlockSpec((tm, MLA_Q_RANK), lambda i: (i, COL_Q // MLA_Q_RANK)),
            pl.BlockSpec((tm, MLA_KV_RANK), lambda i: (i, COL_KV // MLA_KV_RANK)),
            pl.BlockSpec((1, MLA_Q_RANK), lambda i: (0, 0)),
            pl.BlockSpec((1, MLA_KV_RANK), lambda i: (0, 0)),
            pl.BlockSpec(wqn.shape, lambda i: (0, 0)),
            pl.BlockSpec(wqr.shape, lambda i: (0, 0)),
            pl.BlockSpec(wuk.shape, lambda i: (0, 0, 0)),
        ],
        out_specs=[
            pl.BlockSpec((MLA_HEADS, tm, MLA_KV_RANK), lambda i: (0, i, 0)),
            pl.BlockSpec((tm, MLA_HEADS * MLA_ROPE), lambda i: (i, 0)),
            pl.BlockSpec((tm, MLA_KV_RANK), lambda i: (i, 0)),
        ],
        out_shape=[
            jax.ShapeDtypeStruct((MLA_HEADS, n, MLA_KV_RANK), BF16),
            jax.ShapeDtypeStruct((n, MLA_HEADS * MLA_ROPE), F32),
            jax.ShapeDtypeStruct((n, MLA_KV_RANK), F32),
        ],
        compiler_params=_params("parallel"),
        name="mlaprep",
    )(z, z, gq.reshape(1, -1), gkv.reshape(1, -1), wqn, wqr, wuk)


def _mla_prompt_kernel(ql_ref, qr_ref, ckv_ref, kr_ref, wuv_ref, o_ref, m_ref, l_ref, acc_ref, *, tq, kc):
    q0 = pl.program_id(1) * tq
    hh = MLA_HEADS
    m_ref[...] = jnp.full(m_ref.shape, NEG_INF, F32)
    l_ref[...] = jnp.zeros_like(l_ref)
    acc_ref[...] = jnp.zeros_like(acc_ref)
    t = q0 + lax.broadcasted_iota(jnp.int32, (tq, kc), 0)
    koff = lax.broadcasted_iota(jnp.int32, (tq, kc), 1)

    def chunk(c, masked):
        k0 = pl.multiple_of(c * kc, kc)
        kk = ckv_ref[pl.ds(k0, kc), :]
        kr = kr_ref[pl.ds(k0, kc), :]
        hg = MLA_HEAD_GROUP
        for h0 in range(0, hh, hg):
            hs = slice(h0, h0 + hg)
            ql = ql_ref[hs].reshape(hg * tq, MLA_KV_RANK)
            qr = qr_ref[hs].reshape(hg * tq, MLA_ROPE)
            s = (_dot_nt(ql, kk) + _dot_nt(qr, kr)).reshape(hg, tq, kc)
            if masked:
                s = jnp.where(koff + k0 <= t, s, NEG_INF)
            m_prev = m_ref[hs]
            m_new = jnp.maximum(m_prev, jnp.max(s, axis=-1, keepdims=True))
            alpha = jnp.exp2(m_prev - m_new)
            p = jnp.exp2(s - m_new)
            l_ref[hs] = alpha * l_ref[hs] + jnp.sum(p, axis=-1, keepdims=True)
            pv = _dot(p.reshape(hg * tq, kc).astype(BF16), kk)
            acc_ref[hs] = alpha * acc_ref[hs] + pv.reshape(hg, tq, MLA_KV_RANK)
            m_ref[hs] = m_new

    def body(c, carry):
        chunk(c, False)
        return carry

    lax.fori_loop(0, q0 // kc, body, 0)
    chunk(q0 // kc, True)
    for h in range(hh):
        o = (acc_ref[h] / l_ref[h]).astype(BF16)
        o_ref[:, h * MLA_V:(h + 1) * MLA_V] = _dot(o, wuv_ref[h]).astype(o_ref.dtype)


def _mla_prompt(ql, qr, ckv, kr, wuv, nb, seq):
    tq = min(QUERY_BLOCK, seq)
    kc = min(MLA_KEY_CHUNK, seq)
    assert kc % tq == 0
    nq = seq // tq
    hh = MLA_HEADS
    return pl.pallas_call(
        functools.partial(_mla_prompt_kernel, tq=tq, kc=kc),
        grid=(nb, nq),
        in_specs=[
            pl.BlockSpec((hh, tq, MLA_KV_RANK), lambda b, i: (0, b * nq + i, 0)),
            pl.BlockSpec((hh, tq, MLA_ROPE), lambda b, i: (0, b * nq + i, 0)),
            pl.BlockSpec((seq, MLA_KV_RANK), lambda b, i: (b, 0)),
            pl.BlockSpec((seq, MLA_ROPE), lambda b, i: (b, 0)),
            pl.BlockSpec(wuv.shape, lambda b, i: (0, 0, 0)),
        ],
        out_specs=pl.BlockSpec((tq, hh * MLA_V), lambda b, i: (b * nq + i, 0)),
        out_shape=jax.ShapeDtypeStruct((nb * seq, hh * MLA_V), BF16),
        scratch_shapes=[
            pltpu.VMEM((hh, tq, 1), F32),
            pltpu.VMEM((hh, tq, 1), F32),
            pltpu.VMEM((hh, tq, MLA_KV_RANK), F32),
        ],
        compiler_params=_params("parallel", "arbitrary"),
        name="mla_prompt",
    )(ql, qr, ckv, kr, wuv)


SUBLANES = 8
TILES_PER_SEG = NSA_KV_GROUPS * CMP_STRIDE // SUBLANES
SEGS_PER_PAGE = PAGE_ROWS // CMP_STRIDE
PAGE_IROWS = NSA_KV_GROUPS * PAGE_ROWS


def _page_segment_sums(x_ref, row0, wt_ref):
    sub = lax.broadcasted_iota(jnp.int32, (SUBLANES, NSA_DH), 0)
    out = [[jnp.zeros((SEGS_PER_PAGE, NSA_DH), F32) for _ in range(NSA_KV_GROUPS)] for _ in range(2)]
    for n in range(SEGS_PER_PAGE):
        tiles = [x_ref[pl.ds(row0 + (n * TILES_PER_SEG + k) * SUBLANES, SUBLANES), :] for k in range(TILES_PER_SEG)]
        for m in range(2):
            w0 = m * TILES_PER_SEG * SUBLANES
            p = tiles[0] * wt_ref[w0:w0 + SUBLANES, :]
            for k in range(1, TILES_PER_SEG):
                p = p + tiles[k] * wt_ref[w0 + k * SUBLANES:w0 + (k + 1) * SUBLANES, :]
            p = p + pltpu.roll(p, 4, 0)
            p = p + pltpu.roll(p, 2, 0)
            q = pltpu.roll(p, 1, 0)
            for g in range(NSA_KV_GROUPS):
                out[m][g] = jnp.where(sub == n, p if n % 2 == g else q, out[m][g])
    return out


def _compress_prompt_kernel(k_ref, v_ref, wtk_ref, wtv_ref, link_ref, linv_ref, kc_ref, vc_ref, *, nseg):
    npg = nseg // SEGS_PER_PAGE
    for x_ref, wt_ref, lin_ref, o_ref in ((k_ref, wtk_ref, link_ref, kc_ref), (v_ref, wtv_ref, linv_ref, vc_ref)):
        parts = [_page_segment_sums(x_ref, pg * PAGE_IROWS, wt_ref) for pg in range(npg)]
        for g in range(NSA_KV_GROUPS):
            a0 = jnp.concatenate([pt[0][g] for pt in parts], axis=0)
            a1 = jnp.concatenate([pt[1][g] for pt in parts], axis=0)
            acc = (a0 + pltpu.roll(a1, nseg - 1, 0)).astype(BF16)
            o_ref[g] = _dot(acc, lin_ref[...]).astype(BF16)


def _compress_prompt(k, v, pwk, pwv, link, linv, nb, seq):
    nseg = seq // CMP_STRIDE
    row = pl.BlockSpec((NSA_KV_GROUPS * seq, NSA_DH), lambda b: (b, 0))
    pw = pl.BlockSpec((NSA_KV_GROUPS * CMP_BLOCK, NSA_DH), lambda b: (0, 0))
    lin = pl.BlockSpec((NSA_DH, NSA_DH), lambda b: (0, 0))
    out = pl.BlockSpec((None, NSA_KV_GROUPS, nseg, NSA_DH), lambda b: (b, 0, 0, 0))
    shp = jax.ShapeDtypeStruct((nb, NSA_KV_GROUPS, nseg, NSA_DH), BF16)
    return pl.pallas_call(
        functools.partial(_compress_prompt_kernel, nseg=nseg),
        grid=(nb,),
        in_specs=[row, row, pw, pw, lin, lin],
        out_specs=[out, out],
        out_shape=[shp, shp],
        compiler_params=_params("parallel"),
        name="compress_prompt",
    )(k.reshape(-1, NSA_DH), v.reshape(-1, NSA_DH), pwk, pwv, link, linv)


def _masked_softmax(s, mask):
    s = jnp.where(mask, s, NEG_INF)
    e = jnp.where(mask, jnp.exp2(s - jnp.max(s, axis=-1, keepdims=True)), 0.0)
    return e / jnp.maximum(jnp.sum(e, axis=-1, keepdims=True), TINY)


def _forced_importance(imp, blk, cur, ns):
    valid = blk <= cur
    forced = valid & ((blk == 0) | (blk > cur - SEL_LOCAL))
    imp = jnp.where(forced, BIG, jnp.where(valid, imp, -BIG))
    return jnp.where(blk < ns, imp, -2.0 * BIG)


def _nsa_prompt_kernel(qn_ref, kcmp_ref, vcmp_ref, ks_ref, vs_ref, kw_ref, vw_ref, zg_ref, o_ref,
                       m_ref, l_ref, acc_ref, *, tq, seq, kc, ntop):
    g = pl.program_id(1)
    q0 = pl.program_id(2) * tq
    jj = NSA_HPG
    nseg = seq // CMP_STRIDE
    ns = seq // SEL_BLOCK

    q = qn_ref[...].reshape(jj * tq, NSA_DH)
    t2 = q0 + lax.broadcasted_iota(jnp.int32, (tq, nseg), 0)
    cmp_last = lax.broadcasted_iota(jnp.int32, (tq, nseg), 1) * CMP_STRIDE + (CMP_BLOCK - 1)
    pc = _masked_softmax(_dot_nt(q, kcmp_ref[...]).reshape(jj, tq, nseg), cmp_last <= t2)
    o_c = _dot(pc.reshape(jj * tq, nseg).astype(BF16), vcmp_ref[...]).reshape(jj, tq, NSA_DH)
    psum = jnp.sum(pc, axis=0)

    blk_r = lax.broadcasted_iota(jnp.int32, (ns, nseg), 0) * SEL_BLOCK
    c_st = lax.broadcasted_iota(jnp.int32, (ns, nseg), 1) * CMP_STRIDE
    ov_t = jnp.where((c_st < blk_r + SEL_BLOCK) & (c_st + CMP_BLOCK > blk_r), 1.0, 0.0).astype(BF16)
    imp_t = _split3_dot_nt(ov_t, psum)
    blk = lax.broadcasted_iota(jnp.int32, (ns, tq), 0)
    cur = (q0 + lax.broadcasted_iota(jnp.int32, (ns, tq), 1)) >> SEL_SHIFT
    imp_t = _forced_importance(imp_t, blk, cur, ns)
    rank = jnp.zeros((ns, tq), F32)
    for mm in range(ns):
        row = imp_t[mm:mm + 1, :]
        tie = jnp.where(blk > mm, 1.0, 0.0)
        rank = rank + jnp.where(row > imp_t, 1.0, jnp.where(row == imp_t, tie, 0.0))
    sel_t = jnp.where(rank < ntop, 1.0, 0.0)
    sel = jnp.concatenate([sel_t, jnp.zeros((LANE - ns, tq), F32)], axis=0).T.astype(BF16)

    m_ref[...] = jnp.full(m_ref.shape, NEG_INF, F32)
    l_ref[...] = jnp.zeros_like(l_ref)
    acc_ref[...] = jnp.zeros_like(acc_ref)
    tk = q0 + lax.broadcasted_iota(jnp.int32, (tq, kc), 0)
    koff = lax.broadcasted_iota(jnp.int32, (tq, kc), 1)
    e_row = lax.broadcasted_iota(jnp.int32, (LANE, kc), 0)
    e_col = lax.broadcasted_iota(jnp.int32, (LANE, kc), 1) >> SEL_SHIFT

    def body(c, carry):
        k0 = pl.multiple_of(c * kc, kc)
        kk = ks_ref[pl.ds(k0, kc), :]
        vv = vs_ref[pl.ds(k0, kc), :]
        expand = jnp.where(e_row == e_col + c * (kc // SEL_BLOCK), 1.0, 0.0).astype(BF16)
        visible = (_dot(sel, expand) > 0.5) & (koff + k0 <= tk)
        hg = NSA_HEAD_GROUP
        for j0 in range(0, jj, hg):
            js = slice(j0, j0 + hg)
            s = _dot_nt(qn_ref[js].reshape(hg * tq, NSA_DH), kk).reshape(hg, tq, kc)
            s = jnp.where(visible, s, NEG_INF)
            m_prev = m_ref[js]
            m_new = jnp.maximum(m_prev, jnp.max(s, axis=-1, keepdims=True))
            alpha = jnp.exp2(m_prev - m_new)
            p = jnp.exp2(s - m_new)
            l_ref[js] = alpha * l_ref[js] + jnp.sum(p, axis=-1, keepdims=True)
            pv = _dot(p.reshape(hg * tq, kc).astype(BF16), vv)
            acc_ref[js] = alpha * acc_ref[js] + pv.reshape(hg, tq, NSA_DH)
            m_ref[js] = m_new
        return carry

    lax.fori_loop(0, (q0 + tq + kc - 1) // kc, body, 0)

    wl = min(WINDOW + tq, seq)
    w0 = pl.multiple_of(jnp.clip(q0 - WINDOW, 0, seq - wl), tq)
    kw = kw_ref[pl.ds(w0, wl), :]
    vw = vw_ref[pl.ds(w0, wl), :]
    tw = q0 + lax.broadcasted_iota(jnp.int32, (tq, wl), 0)
    wpos = w0 + lax.broadcasted_iota(jnp.int32, (tq, wl), 1)
    win_ok = (wpos >= tw - WINDOW) & (wpos <= tw)

    sw = jnp.where(win_ok, _dot_nt(q, kw).reshape(jj, tq, wl), NEG_INF)
    pw = jnp.exp2(sw - jnp.max(sw, axis=-1, keepdims=True))
    o_w = (_dot(pw.reshape(jj * tq, wl).astype(BF16), vw).reshape(jj, tq, NSA_DH)
           / jnp.sum(pw, axis=-1, keepdims=True))
    o_s = acc_ref[...] / l_ref[...]

    zg = zg_ref[...]
    lane = lax.broadcasted_iota(jnp.int32, zg.shape, 1)
    for j in range(jj):
        def gate(k, j=j):
            col = jnp.sum(jnp.where(lane == COL_G % LANE + (g * jj + j) * 3 + k, zg, 0.0), axis=-1, keepdims=True)
            return 1.0 / (1.0 + jnp.exp(-col))
        o = gate(0) * o_c[j] + gate(1) * o_s[j] + gate(2) * o_w[j]
        o_ref[:, j * NSA_DH:(j + 1) * NSA_DH] = o.astype(o_ref.dtype)


def _nsa_prompt(qn, kcmp, vcmp, ks, vs, kw, vw, z, nb, seq):
    tq = min(QUERY_BLOCK, seq)
    kc = min(NSA_KEY_CHUNK, seq)
    nq = seq // tq
    jj = NSA_HPG
    nseg = seq // CMP_STRIDE
    ntop = min(SEL_TOPK, seq // SEL_BLOCK)
    rows = pl.BlockSpec((seq, NSA_DH), lambda b, g, i: (b, g))
    cmp = pl.BlockSpec((None, None, nseg, NSA_DH), lambda b, g, i: (b, g, 0, 0))
    return pl.pallas_call(
        functools.partial(_nsa_prompt_kernel, tq=tq, seq=seq, kc=kc, ntop=ntop),
        grid=(nb, NSA_KV_GROUPS, nq),
        in_specs=[
            pl.BlockSpec((jj, tq, NSA_DH), lambda b, g, i: (g, b * nq + i, 0)),
            cmp, cmp, rows, rows, rows, rows,
            pl.BlockSpec((tq, LANE), lambda b, g, i: (b * nq + i, COL_G // LANE)),
        ],
        out_specs=pl.BlockSpec((tq, jj * NSA_DH), lambda b, g, i: (b * nq + i, g)),
        out_shape=jax.ShapeDtypeStruct((nb * seq, NSA_HEADS * NSA_DH), BF16),
        scratch_shapes=[
            pltpu.VMEM((jj, tq, 1), F32),
            pltpu.VMEM((jj, tq, 1), F32),
            pltpu.VMEM((jj, tq, NSA_DH), F32),
        ],
        compiler_params=_params("parallel", "parallel", "arbitrary"),
        name="nsa_prompt",
    )(qn, kcmp, vcmp, ks, vs, kw, vw, z)


def _sample_select_kernel(pt_ref, *refs, pps, nch, past, nq, ns, ntop):
    del pt_ref
    kp = refs[:pps]
    vp = refs[pps:2 * pps]
    (qn_ref, pwk_ref, pwv_ref, link_ref, linv_ref, ov_ref, oc_ref, sel_ref,
     a0k_ref, a1k_ref, a0v_ref, a1v_ref) = refs[2 * pps:]
    c = pl.program_id(1)
    segs = SEGS_PER_PAGE
    nseg = nch * pps * segs
    for p in range(pps):
        r0 = pl.multiple_of(c * (pps * segs) + p * segs, segs)
        for pg, pw_ref, a0_ref, a1_ref in ((kp[p], pwk_ref, a0k_ref, a1k_ref), (vp[p], pwv_ref, a0v_ref, a1v_ref)):
            part = _page_segment_sums(pg, 0, pw_ref)
            for g in range(NSA_KV_GROUPS):
                a0_ref[g, pl.ds(r0, segs), :] = part[0][g]
                a1_ref[g, pl.ds(r0, segs), :] = part[1][g]

    @pl.when(c == nch - 1)
    def _():
        qp = QUERY_PAD
        rows = NSA_HPG * qp
        qi = lax.broadcasted_iota(jnp.int32, (1, qp, nseg), 1)
        t3 = past + jnp.minimum(qi, nq - 1)
        cmp_last = lax.broadcasted_iota(jnp.int32, (1, qp, nseg), 2) * CMP_STRIDE + (CMP_BLOCK - 1)
        imps = []
        for g in range(NSA_KV_GROUPS):
            acc_k = (a0k_ref[g] + pltpu.roll(a1k_ref[g], nseg - 1, 0)).astype(BF16)
            acc_v = (a0v_ref[g] + pltpu.roll(a1v_ref[g], nseg - 1, 0)).astype(BF16)
            kc = _dot(acc_k, link_ref[...]).astype(BF16)
            vc = _dot(acc_v, linv_ref[...]).astype(BF16)
            sc = _dot_nt(qn_ref[g], kc).reshape(NSA_HPG, qp, nseg)
            pc = _masked_softmax(sc, cmp_last <= t3)
            oc_ref[g] = _dot(pc.reshape(rows, nseg).astype(BF16), vc)
            imps.append(_split3_dot(jnp.sum(pc, axis=0), ov_ref[...]))
        imp = jnp.concatenate(imps, axis=0)
        selw = imp.shape[1]
        blk = lax.broadcasted_iota(jnp.int32, imp.shape, 1)
        qrow = lax.broadcasted_iota(jnp.int32, imp.shape, 0) & (qp - 1)
        cur = (past + jnp.minimum(qrow, nq - 1)) >> SEL_SHIFT
        work = _forced_importance(imp, blk, cur, ns)
        blk_f = blk.astype(F32)
        chosen = jnp.zeros(imp.shape, F32)
        for _ in range(ntop):
            top = jnp.max(work, axis=-1, keepdims=True)
            first = jnp.min(jnp.where(work == top, blk_f, float(selw)), axis=-1, keepdims=True)
            hit = blk_f == first
            chosen = jnp.where(hit, 1.0, chosen)
            work = jnp.where(hit, -4.0 * BIG, work)
        sel_ref[...] = chosen


def _sample_select(page_table, kpool, vpool, qn, pwk, pwv, link, linv, ov, past, nq, ns, ntop):
    db, npages = page_table.shape
    pps = min(PAGES_PER_STEP, npages)
    nch = npages // pps
    nseg = npages * (PAGE_ROWS // CMP_STRIDE)
    rows = NSA_HPG * QUERY_PAD
    selw = ov.shape[1]
    pages = [pl.BlockSpec((None, NSA_KV_GROUPS * PAGE_ROWS, NSA_DH),
                          lambda b, c, pt, p=p: (pt[b * npages + c * pps + p], 0, 0)) for p in range(pps)]
    const2 = lambda b, c, pt: (0, 0)
    grid_spec = pltpu.PrefetchScalarGridSpec(
        num_scalar_prefetch=1,
        grid=(db, nch),
        in_specs=pages + pages + [
            pl.BlockSpec((None, NSA_KV_GROUPS, rows, NSA_DH), lambda b, c, pt: (b, 0, 0, 0)),
            pl.BlockSpec((NSA_KV_GROUPS * CMP_BLOCK, NSA_DH), const2),
            pl.BlockSpec((NSA_KV_GROUPS * CMP_BLOCK, NSA_DH), const2),
            pl.BlockSpec((NSA_DH, NSA_DH), const2),
            pl.BlockSpec((NSA_DH, NSA_DH), const2),
            pl.BlockSpec(ov.shape, const2),
        ],
        out_specs=[
            pl.BlockSpec((None, NSA_KV_GROUPS, rows, NSA_DH), lambda b, c, pt: (b, 0, 0, 0)),
            pl.BlockSpec((None, NSA_KV_GROUPS * QUERY_PAD, selw), lambda b, c, pt: (b, 0, 0)),
        ],
        scratch_shapes=[pltpu.VMEM((NSA_KV_GROUPS, nseg, NSA_DH), F32) for _ in range(4)],
    )
    return pl.pallas_call(
        functools.partial(_sample_select_kernel, pps=pps, nch=nch, past=past, nq=nq, ns=ns, ntop=ntop),
        grid_spec=grid_spec,
        out_shape=[
            jax.ShapeDtypeStruct((db, NSA_KV_GROUPS, rows, NSA_DH), F32),
            jax.ShapeDtypeStruct((db, NSA_KV_GROUPS * QUERY_PAD, selw), F32),
        ],
        compiler_params=_params("parallel", "arbitrary"),
        name="sample_select",
    )(page_table.reshape(-1), *([kpool] * pps), *([vpool] * pps), qn, pwk, pwv, link, linv, ov)


def _sample_attend_kernel(pt_ref, *refs, pps, nch, past, nq, win_buf):
    del pt_ref
    ckv_p = refs[:pps]
    kr_p = refs[pps:2 * pps]
    ks_p = refs[2 * pps:3 * pps]
    vs_p = refs[3 * pps:4 * pps]
    (ql_ref, qr_ref, qn_ref, selc_ref, sell_ref, oc_ref, gt_ref, ckvn_ref, krn_ref, ksn_ref, vsn_ref,
     kw_ref, vw_ref, wuv_ref, omla_ref, onsa_ref,
     kc_s, kr_s, ks_s, vs_s, m1_ref, l1_ref, acc1_ref, m2_ref, l2_ref, acc2_ref) = refs[4 * pps:]
    c = pl.program_id(1)
    qp = QUERY_PAD
    jj = NSA_HPG
    hh = MLA_HEADS
    kk = pps * PAGE_ROWS

    @pl.when(c == 0)
    def _():
        m1_ref[...] = jnp.full(m1_ref.shape, NEG_INF, F32)
        l1_ref[...] = jnp.zeros_like(l1_ref)
        acc1_ref[...] = jnp.zeros_like(acc1_ref)
        m2_ref[...] = jnp.full(m2_ref.shape, NEG_INF, F32)
        l2_ref[...] = jnp.zeros_like(l2_ref)
        acc2_ref[...] = jnp.zeros_like(acc2_ref)

    for p in range(pps):
        sl = slice(p * PAGE_ROWS, (p + 1) * PAGE_ROWS)
        sl2 = slice(p * PAGE_IROWS, (p + 1) * PAGE_IROWS)
        kc_s[sl, :] = ckv_p[p][...].astype(BF16)
        kr_s[:, sl] = kr_p[p][...].astype(BF16)
        ks_s[sl2, :] = ks_p[p][...].astype(BF16)
        vs_s[sl2, :] = vs_p[p][...].astype(BF16)

    def online(m_ref, l_ref, acc_ref, s, v):
        m_prev = m_ref[...]
        m_new = jnp.maximum(m_prev, jnp.max(s, axis=-1, keepdims=True))
        alpha = jnp.exp2(m_prev - m_new)
        p = jnp.exp2(s - m_new)
        l_ref[...] = alpha * l_ref[...] + jnp.sum(p, axis=-1, keepdims=True)
        acc_ref[...] = alpha * acc_ref[...] + _dot(p.astype(BF16), v)
        m_ref[...] = m_new

    ql = ql_ref[...]
    qr = qr_ref[...]
    kc = kc_s[...]
    s1 = _dot_nt(ql, kc) + _dot(qr, kr_s[...])
    online(m1_ref, l1_ref, acc1_ref, s1, kc)

    gg = NSA_KV_GROUPS
    qn = qn_ref[...]
    selc = selc_ref[...]
    tile_shape = (gg * qp, gg * SEL_BLOCK)
    own = (lax.broadcasted_iota(jnp.int32, tile_shape, 0) >> (qp.bit_length() - 1)
           == (lax.broadcasted_iota(jnp.int32, tile_shape, 1) & (gg - 1)))
    chosen = jnp.concatenate(
        [jnp.where(own, jnp.broadcast_to(selc[:, n:n + 1], tile_shape), 0.0) for n in range(kk // SEL_BLOCK)], axis=1)
    s2 = _dot_nt(qn, ks_s[...]).reshape(gg, jj, qp, gg * kk)
    s2 = jnp.where(chosen.reshape(gg, 1, qp, gg * kk) > 0.5, s2, NEG_INF).reshape(gg * jj * qp, gg * kk)
    online(m2_ref, l2_ref, acc2_ref, s2, vs_s[...])

    @pl.when(c == nch - 1)
    def _():
        npad = NEW_PAD
        kidx = lax.broadcasted_iota(jnp.int32, (1, qp, npad), 2)
        qidx = lax.broadcasted_iota(jnp.int32, (1, qp, npad), 1)
        new_ok = (kidx <= qidx) & (kidx < nq)
        ckvn = ckvn_ref[...]
        s1n = _dot_nt(ql, ckvn) + _dot_nt(qr, krn_ref[...])
        s1n = jnp.where(new_ok, s1n.reshape(hh, qp, npad), NEG_INF).reshape(hh * qp, npad)
        online(m1_ref, l1_ref, acc1_ref, s1n, ckvn)
        o_lat = (acc1_ref[...] / l1_ref[...]).astype(BF16)
        for h in range(hh):
            omla_ref[:, h * MLA_V:(h + 1) * MLA_V] = _dot(o_lat[h * qp:(h + 1) * qp], wuv_ref[h]).astype(omla_ref.dtype)

        wl = kw_ref.shape[0]
        widx = lax.broadcasted_iota(jnp.int32, (1, qp, wl), 2)
        wpos = past - win_buf + widx
        tw = past + jnp.minimum(lax.broadcasted_iota(jnp.int32, (1, qp, wl), 1), nq - 1)
        win_ok = (widx < win_buf + nq) & (wpos >= tw - WINDOW) & (wpos <= tw)
        shp = (gg, 1, qp, gg * npad)
        ncol = lax.broadcasted_iota(jnp.int32, shp, 3)
        nrow_g = lax.broadcasted_iota(jnp.int32, shp, 0)
        nq_i = lax.broadcasted_iota(jnp.int32, shp, 2)
        nkk = ncol & (npad - 1)
        sel_new = (((ncol >> NEW_SHIFT) == nrow_g) & (nkk <= nq_i) & (nkk < nq)
                   & (sell_ref[:, 0:gg * npad].reshape(shp) > 0.5))
        s2n = _dot_nt(qn, ksn_ref[...]).reshape(gg, jj, qp, gg * npad)
        s2n = jnp.where(sel_new, s2n, NEG_INF).reshape(gg * jj * qp, gg * npad)
        online(m2_ref, l2_ref, acc2_ref, s2n, vsn_ref[...])
        o_s_all = acc2_ref[...] / l2_ref[...]
        for g in range(NSA_KV_GROUPS):
            cols = slice(g * NSA_DH, (g + 1) * NSA_DH)
            rws = slice(g * jj * qp, (g + 1) * jj * qp)
            qg = qn[rws]
            o_s = o_s_all[rws]

            sw = _dot_nt(qg, kw_ref[:, cols]).reshape(jj, qp, wl)
            sw = jnp.where(win_ok, sw, NEG_INF).reshape(jj * qp, wl)
            pw = jnp.exp2(sw - jnp.max(sw, axis=-1, keepdims=True))
            pw = pw / jnp.sum(pw, axis=-1, keepdims=True)
            o_w = _dot(pw.astype(BF16), vw_ref[:, cols])

            gt = gt_ref[rws, :]
            o = gt[:, 0:1] * oc_ref[rws, :] + gt[:, 1:2] * o_s + gt[:, 2:3] * o_w
            for j in range(jj):
                hcol = (g * jj + j) * NSA_DH
                onsa_ref[:, hcol:hcol + NSA_DH] = o[j * qp:(j + 1) * qp].astype(onsa_ref.dtype)


def _sample_attend(page_table, ckv_pool, kr_pool, ks_pool, vs_pool, ql, qr, qn, selc, sell, oc, gt,
                   ckvn, krn, ksn, vsn, kw, vw, wuv, past, nq, win_buf):
    db, npages = page_table.shape
    pps = min(PAGES_PER_STEP, npages)
    nch = npages // pps
    kk = pps * PAGE_ROWS
    qp = QUERY_PAD
    hh = MLA_HEADS
    rows = NSA_HPG * qp

    def pages(rows_, width):
        return [pl.BlockSpec((None, rows_, width),
                             lambda b, c, pt, p=p: (pt[b * npages + c * pps + p], 0, 0)) for p in range(pps)]

    def per_seq(shape):
        nd = len(shape)
        return pl.BlockSpec((None,) + tuple(shape), lambda b, c, pt: (b,) + (0,) * nd)

    def const(shape):
        nd = len(shape)
        return pl.BlockSpec(tuple(shape), lambda b, c, pt: (0,) * nd)

    in_specs = (
        pages(PAGE_ROWS, MLA_KV_RANK) + pages(MLA_ROPE, PAGE_ROWS)
        + pages(PAGE_IROWS, NSA_DH) + pages(PAGE_IROWS, NSA_DH) + [
            per_seq((hh * qp, MLA_KV_RANK)),
            per_seq((hh * qp, MLA_ROPE)),
            per_seq((NSA_KV_GROUPS * rows, NSA_DH)),
            pl.BlockSpec((None, None, NSA_KV_GROUPS * qp, LANE), lambda b, c, pt: (b, c, 0, 0)),
            per_seq((NSA_KV_GROUPS * qp, LANE)),
            per_seq((NSA_KV_GROUPS * rows, NSA_DH)),
            per_seq((NSA_KV_GROUPS * rows, 3)),
            per_seq((NEW_PAD, MLA_KV_RANK)),
            per_seq((NEW_PAD, MLA_ROPE)),
            per_seq((NSA_KV_GROUPS * NEW_PAD, NSA_DH)),
            per_seq((NSA_KV_GROUPS * NEW_PAD, NSA_DH)),
            per_seq(kw.shape[1:]),
            per_seq(vw.shape[1:]),
            const(wuv.shape),
        ])
    grid_spec = pltpu.PrefetchScalarGridSpec(
        num_scalar_prefetch=1,
        grid=(db, nch),
        in_specs=in_specs,
        out_specs=[per_seq((qp, hh * MLA_V)), per_seq((qp, NSA_HEADS * NSA_DH))],
        scratch_shapes=[
            pltpu.VMEM((kk, MLA_KV_RANK), BF16),
            pltpu.VMEM((MLA_ROPE, kk), BF16),
            pltpu.VMEM((NSA_KV_GROUPS * kk, NSA_DH), BF16),
            pltpu.VMEM((NSA_KV_GROUPS * kk, NSA_DH), BF16),
            pltpu.VMEM((hh * qp, 1), F32),
            pltpu.VMEM((hh * qp, 1), F32),
            pltpu.VMEM((hh * qp, MLA_KV_RANK), F32),
            pltpu.VMEM((NSA_KV_GROUPS * rows, 1), F32),
            pltpu.VMEM((NSA_KV_GROUPS * rows, 1), F32),
            pltpu.VMEM((NSA_KV_GROUPS * rows, NSA_DH), F32),
        ],
    )
    return pl.pallas_call(
        functools.partial(_sample_attend_kernel, pps=pps, nch=nch, past=past, nq=nq, win_buf=win_buf),
        grid_spec=grid_spec,
        out_shape=[
            jax.ShapeDtypeStruct((db, qp, hh * MLA_V), BF16),
            jax.ShapeDtypeStruct((db, qp, NSA_HEADS * NSA_DH), BF16),
        ],
        compiler_params=_params("parallel", "arbitrary"),
        name="sample_attend",
    )(page_table.reshape(-1), *([ckv_pool] * pps), *([kr_pool] * pps), *([ks_pool] * pps), *([vs_pool] * pps),
      ql, qr, qn, selc, sell, oc, gt, ckvn, krn, ksn, vsn, kw, vw, wuv)


def _outproj_kernel(x_ref, a_ref, b_ref, wa_ref, wb_ref, o_ref):
    o_ref[...] = x_ref[...] + _dot(a_ref[...], wa_ref[...]) + _dot(b_ref[...], wb_ref[...])


def _outproj(x, a, b, wa, wb):
    n, d = x.shape
    tm = _row_tile(n, 512)
    return pl.pallas_call(
        _outproj_kernel,
        grid=(n // tm,),
        in_specs=[
            pl.BlockSpec((tm, d), lambda i: (i, 0)),
            pl.BlockSpec((tm, a.shape[1]), lambda i: (i, 0)),
            pl.BlockSpec((tm, b.shape[1]), lambda i: (i, 0)),
            pl.BlockSpec(wa.shape, lambda i: (0, 0)),
            pl.BlockSpec(wb.shape, lambda i: (0, 0)),
        ],
        out_specs=pl.BlockSpec((tm, d), lambda i: (i, 0)),
        out_shape=jax.ShapeDtypeStruct((n, d), F32),
        compiler_params=_params("parallel"),
        name="outproj",
    )(x, a, b, wa, wb)


def _rope_tables(pos, rot_dim):
    inv = ROPE_THETA ** (-jnp.arange(0, rot_dim, 2, dtype=F32) / rot_dim)
    ang = pos.astype(F32)[:, None] * inv[None, :]
    return jnp.cos(ang), jnp.sin(ang)


def _rope(x, cos, sin):
    half = cos.shape[-1]
    c = cos[:, None, :]
    s = sin[:, None, :]
    x1, x2 = x[..., :half], x[..., half:2 * half]
    return jnp.concatenate([x1 * c - x2 * s, x2 * c + x1 * s, x[..., 2 * half:]], axis=-1)


def _pad_queries(a, db, nq):
    heads, d = a.shape[1:]
    a = a.reshape(db, nq, heads, d).transpose(0, 2, 1, 3)
    a = jnp.pad(a, ((0, 0), (0, 0), (0, QUERY_PAD - nq), (0, 0)))
    return a.reshape(db, heads * QUERY_PAD, d)


def _pad_new(a, db, nq):
    a = a.reshape(db, nq, a.shape[-1])
    return jnp.pad(a, ((0, 0), (0, NEW_PAD - nq), (0, 0))).astype(BF16)


def _pad_new_groups(a, db, nq):
    a = a.reshape(db, nq, NSA_KV_GROUPS, NSA_DH).transpose(0, 2, 1, 3)
    a = jnp.pad(a, ((0, 0), (0, 0), (0, NEW_PAD - nq), (0, 0)))
    return a.reshape(db, NSA_KV_GROUPS * NEW_PAD, NSA_DH).astype(BF16)


def kernel(x_prompt, x_sample, cache_mla_ckv, cache_mla_krope, cache_nsa_k_cmp, cache_nsa_v_cmp, cache_nsa_k_sel, cache_nsa_v_sel, state_nsa_k_win, state_nsa_v_win, page_table, ffn1_norm, w_ffn1_gate, w_ffn1_up, w_ffn1_down, mix_norm, w_in, mla_q_norm, w_mla_q_up, mla_kv_norm, w_mla_k_up, w_mla_v_up, nsa_cmp_pos_k, nsa_cmp_lin_k, nsa_cmp_pos_v, nsa_cmp_lin_v, w_out, ffn2_norm, w_ffn2_gate, w_ffn2_up, w_ffn2_down, final_norm):
    nb, seq, d = x_prompt.shape
    db, nq = x_sample.shape[:2]
    depth, n_pool = cache_mla_ckv.shape[:2]
    npages = page_table.shape[1]
    past = npages * PAGE_ROWS
    win_buf = state_nsa_k_win.shape[2]
    np_tok = nb * seq
    ns_tok = db * nq
    assert nq <= QUERY_PAD and cache_mla_ckv.shape[2] == PAGE_ROWS and seq % QUERY_BLOCK == 0
    hh, gg, dh = MLA_HEADS, NSA_KV_GROUPS, NSA_DH
    x = jnp.concatenate([x_prompt.reshape(np_tok, d), x_sample.reshape(ns_tok, d)], axis=0)
    pos = jnp.concatenate([jnp.tile(jnp.arange(seq), nb), jnp.tile(past + jnp.arange(nq), db)])
    cos_m, sin_m = _rope_tables(pos, MLA_ROPE)
    cos_n, sin_n = _rope_tables(pos, NSA_ROT)

    ns_s = -(-(past + nq) // SEL_BLOCK)
    selw = -(-ns_s // LANE) * LANE
    nseg_s = past // CMP_STRIDE
    c_st = np.arange(nseg_s)[:, None] * CMP_STRIDE
    s_st = np.arange(selw)[None, :] * SEL_BLOCK
    overlap = jnp.asarray(((c_st < s_st + SEL_BLOCK) & (c_st + CMP_BLOCK > s_st)), dtype=BF16)
    pps = min(PAGES_PER_STEP, npages)
    nch = npages // pps
    bpc = pps * PAGE_ROWS // SEL_BLOCK
    assert bpc <= LANE

    rows_p = [[] for _ in range(8)]
    rows_s = [[] for _ in range(8)]
    for l in range(depth):
        wq = w_mla_q_up[l].reshape(MLA_Q_RANK, hh, MLA_NOPE + MLA_ROPE)
        wqn = wq[:, :, :MLA_NOPE].reshape(MLA_Q_RANK, hh * MLA_NOPE).astype(BF16)
        wqr = wq[:, :, MLA_NOPE:].reshape(MLA_Q_RANK, hh * MLA_ROPE).astype(BF16)
        wuk = w_mla_k_up[l].reshape(MLA_KV_RANK, hh, MLA_NOPE).transpose(1, 2, 0).astype(BF16)
        wuv = w_mla_v_up[l].reshape(MLA_KV_RANK, hh, MLA_V).transpose(1, 0, 2).astype(BF16)
        wi = w_in[l]
        o_q, o_kv, o_kr = 0, MLA_Q_RANK, MLA_Q_RANK + MLA_KV_RANK
        o_qn = o_kr + MLA_ROPE
        o_kv6 = o_qn + NSA_HEADS * dh
        o_g = o_kv6 + 6 * KV_COLS
        w_in_p = jnp.concatenate([
            wi[:, o_q:o_kr], wi[:, o_qn:o_g], wi[:, o_kr:o_qn], wi[:, o_g:],
            jnp.zeros((d, IN_PAD - wi.shape[1]), wi.dtype)], axis=1).astype(BF16)
        pwk, pwv = (jnp.repeat(w, NSA_KV_GROUPS, axis=0) for w in (nsa_cmp_pos_k[l], nsa_cmp_pos_v[l]))
        link = nsa_cmp_lin_k[l].astype(BF16)
        linv = nsa_cmp_lin_v[l].astype(BF16)

        x = _ffn(x, ffn1_norm[l], w_ffn1_gate[l].astype(BF16), w_ffn1_up[l].astype(BF16),
                 w_ffn1_down[l].astype(BF16))
        z = _inproj(x, mix_norm[l], w_in_p)
        ql, qr_raw, ckv = _mlaprep(z, mla_q_norm[l], mla_kv_norm[l], wqn, wqr, wuk)

        qr = _rope(qr_raw.reshape(-1, hh, MLA_ROPE), cos_m, sin_m) * MLA_QSCALE
        krope = _rope(z[:, None, COL_KR:COL_KR + MLA_ROPE], cos_m, sin_m)[:, 0]
        qn = _rope(z[:, COL_QN:COL_KC].reshape(-1, NSA_HEADS, dh), cos_n, sin_n) * NSA_QSCALE

        def kv_rot(col):
            return _rope(z[:, col:col + KV_COLS].reshape(-1, gg, dh), cos_n, sin_n).reshape(-1, KV_COLS)

        k_cmp, k_sel, k_win = kv_rot(COL_KC), kv_rot(COL_KS), kv_rot(COL_KW)
        v_cmp, v_sel, v_win = (z[:, c0:c0 + KV_COLS] for c0 in (COL_VC, COL_VS, COL_VW))
        ckv_b, krope_b = ckv.astype(BF16), krope.astype(BF16)
        k_sel_b, v_sel_b, k_win_b, v_win_b = (a.astype(BF16) for a in (k_sel, v_sel, k_win, v_win))

        qr_t = qr.transpose(1, 0, 2).astype(BF16)
        qn_t = qn.transpose(1, 0, 2).astype(BF16)
        o_mla_p = _mla_prompt(ql, qr_t, ckv_b, krope_b, wuv, nb, seq)
        kcmp_p, vcmp_p = _compress_prompt(k_cmp, v_cmp, pwk, pwv, link, linv, nb, seq)
        o_nsa_p = _nsa_prompt(qn_t, kcmp_p, vcmp_p, k_sel_b, v_sel_b, k_win_b, v_win_b, z, nb, seq)

        sm = slice(np_tok, None)
        ql_s = _pad_queries(ql[:, sm].transpose(1, 0, 2), db, nq)
        qr_s = _pad_queries(qr[sm], db, nq).astype(BF16)
        qn_s = _pad_queries(qn[sm], db, nq).astype(BF16)
        gates = jax.nn.sigmoid(z[sm, COL_G:COL_G + GATE_COLS]).reshape(ns_tok, NSA_HEADS, 3)
        gt_s = _pad_queries(gates, db, nq)
        ckv_pool = cache_mla_ckv[l]
        kr_pool = jnp.swapaxes(cache_mla_krope[l], 1, 2)
        kc_pool, vc_pool, ks_pool, vs_pool = (c[l].reshape(n_pool, PAGE_IROWS, dh) for c in (
            cache_nsa_k_cmp, cache_nsa_v_cmp, cache_nsa_k_sel, cache_nsa_v_sel))
        o_c, sel = _sample_select(page_table, kc_pool, vc_pool, qn_s.reshape(db, gg, NSA_HPG * QUERY_PAD, dh),
                                  pwk, pwv, link, linv, overlap, past, nq, ns_s, min(SEL_TOPK, ns_s))
        selc = sel[:, :, :nch * bpc].reshape(db, gg * QUERY_PAD, nch, bpc).transpose(0, 2, 1, 3)
        selc = jnp.pad(selc, ((0, 0), (0, 0), (0, 0), (0, LANE - bpc)))
        sell = jnp.broadcast_to(sel[:, :, nch * bpc:nch * bpc + 1], (db, gg * QUERY_PAD, LANE))
        kwin_full = jnp.concatenate([state_nsa_k_win[l].reshape(db, win_buf, KV_COLS),
                                     k_win[sm].reshape(db, nq, KV_COLS)], axis=1)
        vwin_full = jnp.concatenate([state_nsa_v_win[l].reshape(db, win_buf, KV_COLS),
                                     v_win[sm].reshape(db, nq, KV_COLS)], axis=1)
        wpad = ((0, 0), (0, NEW_PAD - nq), (0, 0))
        o_mla_s, o_nsa_s = _sample_attend(
            page_table, ckv_pool, kr_pool, ks_pool, vs_pool, ql_s, qr_s, qn_s, selc, sell,
            o_c.reshape(db, -1, dh), gt_s,
            _pad_new(ckv[sm], db, nq), _pad_new(krope[sm], db, nq), _pad_new_groups(k_sel[sm], db, nq),
            _pad_new_groups(v_sel[sm], db, nq), jnp.pad(kwin_full, wpad).astype(BF16),
            jnp.pad(vwin_full, wpad).astype(BF16), wuv, past, nq, win_buf)

        mix_a = jnp.concatenate([o_mla_p, o_mla_s[:, :nq].reshape(ns_tok, -1)], axis=0)
        mix_b = jnp.concatenate([o_nsa_p, o_nsa_s[:, :nq].reshape(ns_tok, -1)], axis=0)
        wo = w_out[l].astype(BF16)
        x = _outproj(x, mix_a, mix_b, wo[:hh * MLA_V], wo[hh * MLA_V:])
        x = _ffn(x, ffn2_norm[l], w_ffn2_gate[l].astype(BF16), w_ffn2_up[l].astype(BF16),
                 w_ffn2_down[l].astype(BF16), final_g=final_norm if l == depth - 1 else None)

        win_p = min(WINDOW, seq)
        new = (ckv, krope, k_cmp, v_cmp, k_sel, v_sel)
        for i, a in enumerate(new):
            tail = a.shape[1:] if i < 2 else (gg, dh)
            rows_p[i].append(a[:np_tok].reshape((nb, seq) + tail))
            rows_s[i].append(a[np_tok:].reshape((db, nq) + tail))
        rows_p[6].append(k_win[:np_tok].reshape(nb, seq, gg, dh)[:, -win_p:])
        rows_p[7].append(v_win[:np_tok].reshape(nb, seq, gg, dh)[:, -win_p:])
        rows_s[6].append(kwin_full[:, -win_buf:].reshape(db, win_buf, gg, dh))
        rows_s[7].append(vwin_full[:, -win_buf:].reshape(db, win_buf, gg, dh))

    y_prompt = x[:np_tok].reshape(nb, seq, d)
    y_sample = x[np_tok:].reshape(db, nq, d)
    out = [y_prompt, y_sample]
    for i in range(8):
        out += [jnp.stack(rows_p[i]), jnp.stack(rows_s[i])]
    return tuple(out)
```

```python
import functools

import jax
import jax.numpy as jnp
import numpy as np
from jax import lax
from jax.experimental import pallas as pl
from jax.experimental.pallas import tpu as pltpu

MLA_HEADS = 8
MLA_NOPE = 128
MLA_ROPE = 64
MLA_V = 128
MLA_Q_RANK = 512
MLA_KV_RANK = 256
NSA_HEADS = 8
NSA_KV_GROUPS = 2
NSA_HPG = NSA_HEADS // NSA_KV_GROUPS
NSA_DH = 128
CMP_BLOCK = 32
CMP_STRIDE = 16
SEL_BLOCK = 64
SEL_TOPK = 16
SEL_LOCAL = 2
WINDOW = 512
ROPE_THETA = 500000.0
NSA_ROT = NSA_DH // 4
RMS_EPS = 1e-6
QUERY_BLOCK = 128
MLA_SCALE = (MLA_NOPE + MLA_ROPE) ** -0.5
NSA_SCALE = NSA_DH ** -0.5
LOG2E = 1.4426950408889634
MLA_QSCALE = MLA_SCALE * LOG2E
NSA_QSCALE = NSA_SCALE * LOG2E
NEG_INF = -1e30
BIG = 1e6
TINY = 1e-30
KV_COLS = NSA_KV_GROUPS * NSA_DH
GATE_COLS = 3 * NSA_HEADS

COL_Q = 0
COL_KV = COL_Q + MLA_Q_RANK
COL_QN = COL_KV + MLA_KV_RANK
COL_KC = COL_QN + NSA_HEADS * NSA_DH
COL_VC = COL_KC + KV_COLS
COL_KS = COL_VC + KV_COLS
COL_VS = COL_KS + KV_COLS
COL_KW = COL_VS + KV_COLS
COL_VW = COL_KW + KV_COLS
COL_KR = COL_VW + KV_COLS
COL_G = COL_KR + MLA_ROPE
LANE = 128
IN_PAD = -(-(COL_G + GATE_COLS) // (3 * LANE)) * (3 * LANE)

QUERY_PAD = 8
NEW_PAD = 16
PAGE_ROWS = 128
SEL_SHIFT = SEL_BLOCK.bit_length() - 1
NEW_SHIFT = NEW_PAD.bit_length() - 1
PAGES_PER_STEP = 16
ATTEND_PAGES_PER_STEP = 32
MLA_KEY_CHUNK = 1024
MLA_HEAD_GROUP = 4
NSA_KEY_CHUNK = 1024
NSA_HEAD_GROUP = 4
VMEM_LIMIT = 56 * 1024 * 1024

F32 = jnp.float32
BF16 = jnp.bfloat16


def _dot(a, b):
    return jnp.dot(a, b, preferred_element_type=F32)


def _dot_nt(a, b):
    return lax.dot_general(a, b, (((1,), (1,)), ((), ())), preferred_element_type=F32)


def _rms(x, g):
    return x * lax.rsqrt(jnp.mean(x * x, axis=-1, keepdims=True) + RMS_EPS) * g


def _split3_dot_nt(w_bf16, x):
    hi = x.astype(BF16)
    r1 = x - hi.astype(F32)
    mid = r1.astype(BF16)
    lo = (r1 - mid.astype(F32)).astype(BF16)
    return _dot_nt(w_bf16, hi) + _dot_nt(w_bf16, mid) + _dot_nt(w_bf16, lo)


def _split3_dot(x, w_bf16):
    hi = x.astype(BF16)
    r1 = x - hi.astype(F32)
    mid = r1.astype(BF16)
    lo = (r1 - mid.astype(F32)).astype(BF16)
    return _dot(hi, w_bf16) + _dot(mid, w_bf16) + _dot(lo, w_bf16)


def _row_tile(n, want):
    t = min(want, n)
    while n % t:
        t -= 8
    return t


def _params(*sem):
    return pltpu.CompilerParams(dimension_semantics=sem, vmem_limit_bytes=VMEM_LIMIT)


def _ffn_kernel(x_ref, g_ref, wg_ref, wu_ref, wd_ref, *rest, final):
    if final:
        fg_ref, o_ref, h_ref, acc_ref = rest
    else:
        o_ref, h_ref, acc_ref = rest
    j = pl.program_id(1)

    @pl.when(j == 0)
    def _():
        h_ref[...] = _rms(x_ref[...], g_ref[...]).astype(BF16)
        acc_ref[...] = jnp.zeros_like(acc_ref)

    h = h_ref[...]
    a = _dot(h, wg_ref[...])
    u = _dot(h, wu_ref[...])
    act = (a / (1.0 + jnp.exp(-a))) * u
    acc_ref[...] += _dot(act.astype(BF16), wd_ref[...])

    @pl.when(j == pl.num_programs(1) - 1)
    def _():
        y = x_ref[...] + 0.5 * acc_ref[...]
        if final:
            y = _rms(y, fg_ref[...])
        o_ref[...] = y


def _ffn(x, g, wg, wu, wd, final_g=None):
    n, d = x.shape
    dff = wg.shape[1]
    tm = _row_tile(n, 512)
    tf = 512 if dff % 512 == 0 else dff
    in_specs = [
        pl.BlockSpec((tm, d), lambda i, j: (i, 0)),
        pl.BlockSpec((1, d), lambda i, j: (0, 0)),
        pl.BlockSpec((d, tf), lambda i, j: (0, j)),
        pl.BlockSpec((d, tf), lambda i, j: (0, j)),
        pl.BlockSpec((tf, d), lambda i, j: (j, 0)),
    ]
    args = [x, g.reshape(1, d), wg, wu, wd]
    if final_g is not None:
        in_specs.append(pl.BlockSpec((1, d), lambda i, j: (0, 0)))
        args.append(final_g.reshape(1, d))
    return pl.pallas_call(
        functools.partial(_ffn_kernel, final=final_g is not None),
        grid=(n // tm, dff // tf),
        in_specs=in_specs,
        out_specs=pl.BlockSpec((tm, d), lambda i, j: (i, 0)),
        out_shape=jax.ShapeDtypeStruct((n, d), F32),
        scratch_shapes=[pltpu.VMEM((tm, d), BF16), pltpu.VMEM((tm, d), F32)],
        compiler_params=_params("parallel", "arbitrary"),
        name="ffn",
    )(*args)


def _inproj_kernel(x_ref, g_ref, w_ref, o_ref, h_ref):
    @pl.when(pl.program_id(1) == 0)
    def _():
        h_ref[...] = _rms(x_ref[...], g_ref[...]).astype(BF16)

    o_ref[...] = _dot(h_ref[...], w_ref[...])


def _inproj(x, g, w):
    n, d = x.shape
    nout = w.shape[1]
    tm = _row_tile(n, 512)
    tn = nout // 3
    return pl.pallas_call(
        _inproj_kernel,
        grid=(n // tm, nout // tn),
        in_specs=[
            pl.BlockSpec((tm, d), lambda i, j: (i, 0)),
            pl.BlockSpec((1, d), lambda i, j: (0, 0)),
            pl.BlockSpec((d, tn), lambda i, j: (0, j)),
        ],
        out_specs=pl.BlockSpec((tm, tn), lambda i, j: (i, j)),
        out_shape=jax.ShapeDtypeStruct((n, nout), F32),
        scratch_shapes=[pltpu.VMEM((tm, d), BF16)],
        compiler_params=_params("parallel", "arbitrary"),
        name="inproj",
    )(x, g.reshape(1, d), w)


def _mlaprep_kernel(zq_ref, zkv_ref, gq_ref, gkv_ref, wqn_ref, wqr_ref, wuk_ref, ql_ref, qr_ref, ckv_ref):
    cq = _rms(zq_ref[...], gq_ref[...]).astype(BF16)
    qn = _dot(cq, wqn_ref[...]).astype(BF16)
    for h in range(MLA_HEADS):
        ql_ref[h] = (_dot(qn[:, h * MLA_NOPE:(h + 1) * MLA_NOPE], wuk_ref[h]) * MLA_QSCALE).astype(BF16)
    qr_ref[...] = _dot(cq, wqr_ref[...])
    ckv_ref[...] = _rms(zkv_ref[...], gkv_ref[...])


def _mlaprep(z, gq, gkv, wqn, wqr, wuk):
    n = z.shape[0]
    tm = _row_tile(n, 512)
    return pl.pallas_call(
        _mlaprep_kernel,
        grid=(n // tm,),
        in_specs=[
            pl.BlockSpec((tm, MLA_Q_RANK), lambda i: (i, COL_Q // MLA_Q_RANK)),
            pl.BlockSpec((tm, MLA_KV_RANK), lambda i: (i, COL_KV // MLA_KV_RANK)),
            pl.BlockSpec((1, MLA_Q_RANK), lambda i: (0, 0)),
            pl.BlockSpec((1, MLA_KV_RANK), lambda i: (0, 0)),
            pl.BlockSpec(wqn.shape, lambda i: (0, 0)),
            pl.BlockSpec(wqr.shape, lambda i: (0, 0)),
            pl.BlockSpec(wuk.shape, lambda i: (0, 0, 0)),
        ],
        out_specs=[
            pl.BlockSpec((MLA_HEADS, tm, MLA_KV_RANK), lambda i: (0, i, 0)),
            pl.BlockSpec((tm, MLA_HEADS * MLA_ROPE), lambda i: (i, 0)),
            pl.BlockSpec((tm, MLA_KV_RANK), lambda i: (i, 0)),
        ],
        out_shape=[
            jax.ShapeDtypeStruct((MLA_HEADS, n, MLA_KV_RANK), BF16),
            jax.ShapeDtypeStruct((n, MLA_HEADS * MLA_ROPE), F32),
            jax.ShapeDtypeStruct((n, MLA_KV_RANK), F32),
        ],
        compiler_params=_params("parallel"),
        name="mlaprep",
    )(z, z, gq.reshape(1, -1), gkv.reshape(1, -1), wqn, wqr, wuk)


def _mla_prompt_kernel(ql_ref, qr_ref, ckv_ref, kr_ref, wuv_ref, o_ref, m_ref, l_ref, acc_ref, *, tq, kc):
    q0 = pl.program_id(1) * tq
    hh = MLA_HEADS
    m_ref[...] = jnp.full(m_ref.shape, NEG_INF, F32)
    l_ref[...] = jnp.zeros_like(l_ref)
    acc_ref[...] = jnp.zeros_like(acc_ref)
    t = q0 + lax.broadcasted_iota(jnp.int32, (tq, kc), 0)
    koff = lax.broadcasted_iota(jnp.int32, (tq, kc), 1)

    def chunk(c, masked):
        k0 = pl.multiple_of(c * kc, kc)
        kk = ckv_ref[pl.ds(k0, kc), :]
        kr = kr_ref[pl.ds(k0, kc), :]
        hg = MLA_HEAD_GROUP
        for h0 in range(0, hh, hg):
            hs = slice(h0, h0 + hg)
            ql = ql_ref[hs].reshape(hg * tq, MLA_KV_RANK)
            qr = qr_ref[hs].reshape(hg * tq, MLA_ROPE)
            s = (_dot_nt(ql, kk) + _dot_nt(qr, kr)).reshape(hg, tq, kc)
            if masked:
                s = jnp.where(koff + k0 <= t, s, NEG_INF)
            m_prev = m_ref[hs]
            m_new = jnp.maximum(m_prev, jnp.max(s, axis=-1, keepdims=True))
            alpha = jnp.exp2(m_prev - m_new)
            p = jnp.exp2(s - m_new)
            l_ref[hs] = alpha * l_ref[hs] + jnp.sum(p, axis=-1, keepdims=True)
            pv = _dot(p.reshape(hg * tq, kc).astype(BF16), kk)
            acc_ref[hs] = alpha * acc_ref[hs] + pv.reshape(hg, tq, MLA_KV_RANK)
            m_ref[hs] = m_new

    def body(c, carry):
        chunk(c, False)
        return carry

    lax.fori_loop(0, q0 // kc, body, 0)
    chunk(q0 // kc, True)
    for h in range(hh):
        o = (acc_ref[h] / l_ref[h]).astype(BF16)
        o_ref[:, h * MLA_V:(h + 1) * MLA_V] = _dot(o, wuv_ref[h]).astype(o_ref.dtype)


def _mla_prompt(ql, qr, ckv, kr, wuv, nb, seq):
    tq = min(QUERY_BLOCK, seq)
    kc = min(MLA_KEY_CHUNK, seq)
    assert kc % tq == 0
    nq = seq // tq
    hh = MLA_HEADS
    return pl.pallas_call(
        functools.partial(_mla_prompt_kernel, tq=tq, kc=kc),
        grid=(nb, nq),
        in_specs=[
            pl.BlockSpec((hh, tq, MLA_KV_RANK), lambda b, i: (0, b * nq + i, 0)),
            pl.BlockSpec((hh, tq, MLA_ROPE), lambda b, i: (0, b * nq + i, 0)),
            pl.BlockSpec((seq, MLA_KV_RANK), lambda b, i: (b, 0)),
            pl.BlockSpec((seq, MLA_ROPE), lambda b, i: (b, 0)),
            pl.BlockSpec(wuv.shape, lambda b, i: (0, 0, 0)),
        ],
        out_specs=pl.BlockSpec((tq, hh * MLA_V), lambda b, i: (b * nq + i, 0)),
        out_shape=jax.ShapeDtypeStruct((nb * seq, hh * MLA_V), BF16),
        scratch_shapes=[
            pltpu.VMEM((hh, tq, 1), F32),
            pltpu.VMEM((hh, tq, 1), F32),
            pltpu.VMEM((hh, tq, MLA_KV_RANK), F32),
        ],
        compiler_params=_params("parallel", "arbitrary"),
        name="mla_prompt",
    )(ql, qr, ckv, kr, wuv)


SUBLANES = 8
TILES_PER_SEG = NSA_KV_GROUPS * CMP_STRIDE // SUBLANES
SEGS_PER_PAGE = PAGE_ROWS // CMP_STRIDE
PAGE_IROWS = NSA_KV_GROUPS * PAGE_ROWS


def _page_segment_sums(x_ref, row0, wt_ref):
    sub = lax.broadcasted_iota(jnp.int32, (SUBLANES, NSA_DH), 0)
    out = [[jnp.zeros((SEGS_PER_PAGE, NSA_DH), F32) for _ in range(NSA_KV_GROUPS)] for _ in range(2)]
    for n in range(SEGS_PER_PAGE):
        tiles = [x_ref[pl.ds(row0 + (n * TILES_PER_SEG + k) * SUBLANES, SUBLANES), :] for k in range(TILES_PER_SEG)]
        for m in range(2):
            w0 = m * TILES_PER_SEG * SUBLANES
            p = tiles[0] * wt_ref[w0:w0 + SUBLANES, :]
            for k in range(1, TILES_PER_SEG):
                p = p + tiles[k] * wt_ref[w0 + k * SUBLANES:w0 + (k + 1) * SUBLANES, :]
            p = p + pltpu.roll(p, 4, 0)
            p = p + pltpu.roll(p, 2, 0)
            q = pltpu.roll(p, 1, 0)
            for g in range(NSA_KV_GROUPS):
                out[m][g] = jnp.where(sub == n, p if n % 2 == g else q, out[m][g])
    return out


def _compress_prompt_kernel(k_ref, v_ref, wtk_ref, wtv_ref, link_ref, linv_ref, kc_ref, vc_ref, *, nseg):
    npg = nseg // SEGS_PER_PAGE
    for x_ref, wt_ref, lin_ref, o_ref in ((k_ref, wtk_ref, link_ref, kc_ref), (v_ref, wtv_ref, linv_ref, vc_ref)):
        parts = [_page_segment_sums(x_ref, pg * PAGE_IROWS, wt_ref) for pg in range(npg)]
        for g in range(NSA_KV_GROUPS):
            a0 = jnp.concatenate([pt[0][g] for pt in parts], axis=0)
            a1 = jnp.concatenate([pt[1][g] for pt in parts], axis=0)
            acc = (a0 + pltpu.roll(a1, nseg - 1, 0)).astype(BF16)
            o_ref[g] = _dot(acc, lin_ref[...]).astype(BF16)


def _compress_prompt(k, v, pwk, pwv, link, linv, nb, seq):
    nseg = seq // CMP_STRIDE
    row = pl.BlockSpec((NSA_KV_GROUPS * seq, NSA_DH), lambda b: (b, 0))
    pw = pl.BlockSpec((NSA_KV_GROUPS * CMP_BLOCK, NSA_DH), lambda b: (0, 0))
    lin = pl.BlockSpec((NSA_DH, NSA_DH), lambda b: (0, 0))
    out = pl.BlockSpec((None, NSA_KV_GROUPS, nseg, NSA_DH), lambda b: (b, 0, 0, 0))
    shp = jax.ShapeDtypeStruct((nb, NSA_KV_GROUPS, nseg, NSA_DH), BF16)
    return pl.pallas_call(
        functools.partial(_compress_prompt_kernel, nseg=nseg),
        grid=(nb,),
        in_specs=[row, row, pw, pw, lin, lin],
        out_specs=[out, out],
        out_shape=[shp, shp],
        compiler_params=_params("parallel"),
        name="compress_prompt",
    )(k.reshape(-1, NSA_DH), v.reshape(-1, NSA_DH), pwk, pwv, link, linv)


def _masked_softmax(s, mask):
    s = jnp.where(mask, s, NEG_INF)
    e = jnp.where(mask, jnp.exp2(s - jnp.max(s, axis=-1, keepdims=True)), 0.0)
    return e / jnp.maximum(jnp.sum(e, axis=-1, keepdims=True), TINY)


def _forced_importance(imp, blk, cur, ns):
    valid = blk <= cur
    forced = valid & ((blk == 0) | (blk > cur - SEL_LOCAL))
    imp = jnp.where(forced, BIG, jnp.where(valid, imp, -BIG))
    return jnp.where(blk < ns, imp, -2.0 * BIG)


def _nsa_prompt_kernel(qn_ref, kcmp_ref, vcmp_ref, ks_ref, vs_ref, kw_ref, vw_ref, zg_ref, o_ref,
                       m_ref, l_ref, acc_ref, *, tq, seq, kc, ntop):
    g = pl.program_id(1)
    q0 = pl.program_id(2) * tq
    jj = NSA_HPG
    nseg = seq // CMP_STRIDE
    ns = seq // SEL_BLOCK

    q = qn_ref[...].reshape(jj * tq, NSA_DH)
    t2 = q0 + lax.broadcasted_iota(jnp.int32, (tq, nseg), 0)
    cmp_last = lax.broadcasted_iota(jnp.int32, (tq, nseg), 1) * CMP_STRIDE + (CMP_BLOCK - 1)
    pc = _masked_softmax(_dot_nt(q, kcmp_ref[...]).reshape(jj, tq, nseg), cmp_last <= t2)
    o_c = _dot(pc.reshape(jj * tq, nseg).astype(BF16), vcmp_ref[...]).reshape(jj, tq, NSA_DH)
    psum = jnp.sum(pc, axis=0)

    blk_r = lax.broadcasted_iota(jnp.int32, (ns, nseg), 0) * SEL_BLOCK
    c_st = lax.broadcasted_iota(jnp.int32, (ns, nseg), 1) * CMP_STRIDE
    ov_t = jnp.where((c_st < blk_r + SEL_BLOCK) & (c_st + CMP_BLOCK > blk_r), 1.0, 0.0).astype(BF16)
    imp_t = _split3_dot_nt(ov_t, psum)
    blk = lax.broadcasted_iota(jnp.int32, (ns, tq), 0)
    cur = (q0 + lax.broadcasted_iota(jnp.int32, (ns, tq), 1)) >> SEL_SHIFT
    imp_t = _forced_importance(imp_t, blk, cur, ns)
    rank = jnp.zeros((ns, tq), F32)
    for mm in range(ns):
        row = imp_t[mm:mm + 1, :]
        tie = jnp.where(blk > mm, 1.0, 0.0)
        rank = rank + jnp.where(row > imp_t, 1.0, jnp.where(row == imp_t, tie, 0.0))
    sel_t = jnp.where(rank < ntop, 1.0, 0.0)
    sel = jnp.concatenate([sel_t, jnp.zeros((LANE - ns, tq), F32)], axis=0).T.astype(BF16)

    m_ref[...] = jnp.full(m_ref.shape, NEG_INF, F32)
    l_ref[...] = jnp.zeros_like(l_ref)
    acc_ref[...] = jnp.zeros_like(acc_ref)
    tk = q0 + lax.broadcasted_iota(jnp.int32, (tq, kc), 0)
    koff = lax.broadcasted_iota(jnp.int32, (tq, kc), 1)
    e_row = lax.broadcasted_iota(jnp.int32, (LANE, kc), 0)
    e_col = lax.broadcasted_iota(jnp.int32, (LANE, kc), 1) >> SEL_SHIFT

    def body(c, carry):
        k0 = pl.multiple_of(c * kc, kc)
        kk = ks_ref[pl.ds(k0, kc), :]
        vv = vs_ref[pl.ds(k0, kc), :]
        expand = jnp.where(e_row == e_col + c * (kc // SEL_BLOCK), 1.0, 0.0).astype(BF16)
        visible = (_dot(sel, expand) > 0.5) & (koff + k0 <= tk)
        hg = NSA_HEAD_GROUP
        for j0 in range(0, jj, hg):
            js = slice(j0, j0 + hg)
            s = _dot_nt(qn_ref[js].reshape(hg * tq, NSA_DH), kk).reshape(hg, tq, kc)
            s = jnp.where(visible, s, NEG_INF)
            m_prev = m_ref[js]
            m_new = jnp.maximum(m_prev, jnp.max(s, axis=-1, keepdims=True))
            alpha = jnp.exp2(m_prev - m_new)
            p = jnp.exp2(s - m_new)
            l_ref[js] = alpha * l_ref[js] + jnp.sum(p, axis=-1, keepdims=True)
            pv = _dot(p.reshape(hg * tq, kc).astype(BF16), vv)
            acc_ref[js] = alpha * acc_ref[js] + pv.reshape(hg, tq, NSA_DH)
            m_ref[js] = m_new
        return carry

    lax.fori_loop(0, (q0 + tq + kc - 1) // kc, body, 0)

    wl = min(WINDOW + tq, seq)
    w0 = pl.multiple_of(jnp.clip(q0 - WINDOW, 0, seq - wl), tq)
    kw = kw_ref[pl.ds(w0, wl), :]
    vw = vw_ref[pl.ds(w0, wl), :]
    tw = q0 + lax.broadcasted_iota(jnp.int32, (tq, wl), 0)
    wpos = w0 + lax.broadcasted_iota(jnp.int32, (tq, wl), 1)
    win_ok = (wpos >= tw - WINDOW) & (wpos <= tw)

    sw = jnp.where(win_ok, _dot_nt(q, kw).reshape(jj, tq, wl), NEG_INF)
    pw = jnp.exp2(sw - jnp.max(sw, axis=-1, keepdims=True))
    o_w = (_dot(pw.reshape(jj * tq, wl).astype(BF16), vw).reshape(jj, tq, NSA_DH)
           / jnp.sum(pw, axis=-1, keepdims=True))
    o_s = acc_ref[...] / l_ref[...]

    zg = zg_ref[...]
    lane = lax.broadcasted_iota(jnp.int32, zg.shape, 1)
    for j in range(jj):
        def gate(k, j=j):
            col = jnp.sum(jnp.where(lane == COL_G % LANE + (g * jj + j) * 3 + k, zg, 0.0), axis=-1, keepdims=True)
            return 1.0 / (1.0 + jnp.exp(-col))
        o = gate(0) * o_c[j] + gate(1) * o_s[j] + gate(2) * o_w[j]
        o_ref[:, j * NSA_DH:(j + 1) * NSA_DH] = o.astype(o_ref.dtype)


def _nsa_prompt(qn, kcmp, vcmp, ks, vs, kw, vw, z, nb, seq):
    tq = min(QUERY_BLOCK, seq)
    kc = min(NSA_KEY_CHUNK, seq)
    nq = seq // tq
    jj = NSA_HPG
    nseg = seq // CMP_STRIDE
    ntop = min(SEL_TOPK, seq // SEL_BLOCK)
    rows = pl.BlockSpec((seq, NSA_DH), lambda b, g, i: (b, g))
    cmp = pl.BlockSpec((None, None, nseg, NSA_DH), lambda b, g, i: (b, g, 0, 0))
    return pl.pallas_call(
        functools.partial(_nsa_prompt_kernel, tq=tq, seq=seq, kc=kc, ntop=ntop),
        grid=(nb, NSA_KV_GROUPS, nq),
        in_specs=[
            pl.BlockSpec((jj, tq, NSA_DH), lambda b, g, i: (g, b * nq + i, 0)),
            cmp, cmp, rows, rows, rows, rows,
            pl.BlockSpec((tq, LANE), lambda b, g, i: (b * nq + i, COL_G // LANE)),
        ],
        out_specs=pl.BlockSpec((tq, jj * NSA_DH), lambda b, g, i: (b * nq + i, g)),
        out_shape=jax.ShapeDtypeStruct((nb * seq, NSA_HEADS * NSA_DH), BF16),
        scratch_shapes=[
            pltpu.VMEM((jj, tq, 1), F32),
            pltpu.VMEM((jj, tq, 1), F32),
            pltpu.VMEM((jj, tq, NSA_DH), F32),
        ],
        compiler_params=_params("parallel", "parallel", "arbitrary"),
        name="nsa_prompt",
    )(qn, kcmp, vcmp, ks, vs, kw, vw, z)


def _sample_select_kernel(pt_ref, *refs, pps, nch, past, nq, ns, ntop):
    del pt_ref
    kp = refs[:pps]
    vp = refs[pps:2 * pps]
    (qn_ref, pwk_ref, pwv_ref, link_ref, linv_ref, ov_ref, oc_ref, sel_ref,
     a0k_ref, a1k_ref, a0v_ref, a1v_ref) = refs[2 * pps:]
    c = pl.program_id(1)
    segs = SEGS_PER_PAGE
    nseg = nch * pps * segs
    for p in range(pps):
        r0 = pl.multiple_of(c * (pps * segs) + p * segs, segs)
        for pg, pw_ref, a0_ref, a1_ref in ((kp[p], pwk_ref, a0k_ref, a1k_ref), (vp[p], pwv_ref, a0v_ref, a1v_ref)):
            part = _page_segment_sums(pg, 0, pw_ref)
            for g in range(NSA_KV_GROUPS):
                a0_ref[g, pl.ds(r0, segs), :] = part[0][g]
                a1_ref[g, pl.ds(r0, segs), :] = part[1][g]

    @pl.when(c == nch - 1)
    def _():
        qp = QUERY_PAD
        rows = NSA_HPG * qp
        qi = lax.broadcasted_iota(jnp.int32, (1, qp, nseg), 1)
        t3 = past + jnp.minimum(qi, nq - 1)
        cmp_last = lax.broadcasted_iota(jnp.int32, (1, qp, nseg), 2) * CMP_STRIDE + (CMP_BLOCK - 1)
        imps = []
        for g in range(NSA_KV_GROUPS):
            acc_k = (a0k_ref[g] + pltpu.roll(a1k_ref[g], nseg - 1, 0)).astype(BF16)
            acc_v = (a0v_ref[g] + pltpu.roll(a1v_ref[g], nseg - 1, 0)).astype(BF16)
            kc = _dot(acc_k, link_ref[...]).astype(BF16)
            vc = _dot(acc_v, linv_ref[...]).astype(BF16)
            sc = _dot_nt(qn_ref[g], kc).reshape(NSA_HPG, qp, nseg)
            pc = _masked_softmax(sc, cmp_last <= t3)
            oc_ref[g] = _dot(pc.reshape(rows, nseg).astype(BF16), vc)
            imps.append(_split3_dot(jnp.sum(pc, axis=0), ov_ref[...]))
        imp = jnp.concatenate(imps, axis=0)
        selw = imp.shape[1]
        blk = lax.broadcasted_iota(jnp.int32, imp.shape, 1)
        qrow = lax.broadcasted_iota(jnp.int32, imp.shape, 0) & (qp - 1)
        cur = (past + jnp.minimum(qrow, nq - 1)) >> SEL_SHIFT
        work = _forced_importance(imp, blk, cur, ns)
        blk_f = blk.astype(F32)
        chosen = jnp.zeros(imp.shape, F32)
        for _ in range(ntop):
            top = jnp.max(work, axis=-1, keepdims=True)
            first = jnp.min(jnp.where(work == top, blk_f, float(selw)), axis=-1, keepdims=True)
            hit = blk_f == first
            chosen = jnp.where(hit, 1.0, chosen)
            work = jnp.where(hit, -4.0 * BIG, work)
        sel_ref[...] = chosen


def _sample_select(page_table, kpool, vpool, qn, pwk, pwv, link, linv, ov, past, nq, ns, ntop):
    db, npages = page_table.shape
    pps = min(PAGES_PER_STEP, npages)
    nch = npages // pps
    nseg = npages * (PAGE_ROWS // CMP_STRIDE)
    rows = NSA_HPG * QUERY_PAD
    selw = ov.shape[1]
    pages = [pl.BlockSpec((None, NSA_KV_GROUPS * PAGE_ROWS, NSA_DH),
                          lambda b, c, pt, p=p: (pt[b * npages + c * pps + p], 0, 0)) for p in range(pps)]
    const2 = lambda b, c, pt: (0, 0)
    grid_spec = pltpu.PrefetchScalarGridSpec(
        num_scalar_prefetch=1,
        grid=(db, nch),
        in_specs=pages + pages + [
            pl.BlockSpec((None, NSA_KV_GROUPS, rows, NSA_DH), lambda b, c, pt: (b, 0, 0, 0)),
            pl.BlockSpec((NSA_KV_GROUPS * CMP_BLOCK, NSA_DH), const2),
            pl.BlockSpec((NSA_KV_GROUPS * CMP_BLOCK, NSA_DH), const2),
            pl.BlockSpec((NSA_DH, NSA_DH), const2),
            pl.BlockSpec((NSA_DH, NSA_DH), const2),
            pl.BlockSpec(ov.shape, const2),
        ],
        out_specs=[
            pl.BlockSpec((None, NSA_KV_GROUPS, rows, NSA_DH), lambda b, c, pt: (b, 0, 0, 0)),
            pl.BlockSpec((None, NSA_KV_GROUPS * QUERY_PAD, selw), lambda b, c, pt: (b, 0, 0)),
        ],
        scratch_shapes=[pltpu.VMEM((NSA_KV_GROUPS, nseg, NSA_DH), F32) for _ in range(4)],
    )
    return pl.pallas_call(
        functools.partial(_sample_select_kernel, pps=pps, nch=nch, past=past, nq=nq, ns=ns, ntop=ntop),
        grid_spec=grid_spec,
        out_shape=[
            jax.ShapeDtypeStruct((db, NSA_KV_GROUPS, rows, NSA_DH), F32),
            jax.ShapeDtypeStruct((db, NSA_KV_GROUPS * QUERY_PAD, selw), F32),
        ],
        compiler_params=_params("parallel", "arbitrary"),
        name="sample_select",
    )(page_table.reshape(-1), *([kpool] * pps), *([vpool] * pps), qn, pwk, pwv, link, linv, ov)


def _sample_attend_kernel(pt_ref, *refs, pps, nch, past, nq, win_buf):
    del pt_ref
    ckv_p = refs[:pps]
    kr_p = refs[pps:2 * pps]
    ks_p = refs[2 * pps:3 * pps]
    vs_p = refs[3 * pps:4 * pps]
    (ql_ref, qr_ref, qn_ref, selc_ref, sell_ref, oc_ref, gt_ref, ckvn_ref, krn_ref, ksn_ref, vsn_ref,
     kw_ref, vw_ref, wuv_ref, omla_ref, onsa_ref,
     kc_s, kr_s, ks_s, vs_s, m1_ref, l1_ref, acc1_ref, m2_ref, l2_ref, acc2_ref) = refs[4 * pps:]
    c = pl.program_id(1)
    qp = QUERY_PAD
    jj = NSA_HPG
    hh = MLA_HEADS
    kk = pps * PAGE_ROWS

    @pl.when(c == 0)
    def _():
        m1_ref[...] = jnp.full(m1_ref.shape, NEG_INF, F32)
        l1_ref[...] = jnp.zeros_like(l1_ref)
        acc1_ref[...] = jnp.zeros_like(acc1_ref)
        m2_ref[...] = jnp.full(m2_ref.shape, NEG_INF, F32)
        l2_ref[...] = jnp.zeros_like(l2_ref)
        acc2_ref[...] = jnp.zeros_like(acc2_ref)

    for p in range(pps):
        sl = slice(p * PAGE_ROWS, (p + 1) * PAGE_ROWS)
        sl2 = slice(p * PAGE_IROWS, (p + 1) * PAGE_IROWS)
        kc_s[sl, :] = ckv_p[p][...].astype(BF16)
        kr_s[:, sl] = kr_p[p][...].astype(BF16)
        ks_s[sl2, :] = ks_p[p][...].astype(BF16)
        vs_s[sl2, :] = vs_p[p][...].astype(BF16)

    def online(m_ref, l_ref, acc_ref, s, v):
        m_prev = m_ref[...]
        m_new = jnp.maximum(m_prev, jnp.max(s, axis=-1, keepdims=True))
        alpha = jnp.exp2(m_prev - m_new)
        p = jnp.exp2(s - m_new)
        l_ref[...] = alpha * l_ref[...] + jnp.sum(p, axis=-1, keepdims=True)
        acc_ref[...] = alpha * acc_ref[...] + _dot(p.astype(BF16), v)
        m_ref[...] = m_new

    ql = ql_ref[...]
    qr = qr_ref[...]
    kc = kc_s[...]
    s1 = _dot_nt(ql, kc) + _dot(qr, kr_s[...])
    online(m1_ref, l1_ref, acc1_ref, s1, kc)

    gg = NSA_KV_GROUPS
    qn = qn_ref[...]
    selc = selc_ref[...]
    tile_shape = (gg * qp, gg * SEL_BLOCK)
    own = (lax.broadcasted_iota(jnp.int32, tile_shape, 0) >> (qp.bit_length() - 1)
           == (lax.broadcasted_iota(jnp.int32, tile_shape, 1) & (gg - 1)))
    chosen = jnp.concatenate(
        [jnp.where(own, jnp.broadcast_to(selc[:, n:n + 1], tile_shape), 0.0) for n in range(kk // SEL_BLOCK)], axis=1)
    s2 = _dot_nt(qn, ks_s[...]).reshape(gg, jj, qp, gg * kk)
    s2 = jnp.where(chosen.reshape(gg, 1, qp, gg * kk) > 0.5, s2, NEG_INF).reshape(gg * jj * qp, gg * kk)
    online(m2_ref, l2_ref, acc2_ref, s2, vs_s[...])

    @pl.when(c == nch - 1)
    def _():
        npad = NEW_PAD
        kidx = lax.broadcasted_iota(jnp.int32, (1, qp, npad), 2)
        qidx = lax.broadcasted_iota(jnp.int32, (1, qp, npad), 1)
        new_ok = (kidx <= qidx) & (kidx < nq)
        ckvn = ckvn_ref[...]
        s1n = _dot_nt(ql, ckvn) + _dot_nt(qr, krn_ref[...])
        s1n = jnp.where(new_ok, s1n.reshape(hh, qp, npad), NEG_INF).reshape(hh * qp, npad)
        online(m1_ref, l1_ref, acc1_ref, s1n, ckvn)
        o_lat = (acc1_ref[...] / l1_ref[...]).astype(BF16)
        for h in range(hh):
            omla_ref[:, h * MLA_V:(h + 1) * MLA_V] = _dot(o_lat[h * qp:(h + 1) * qp], wuv_ref[h]).astype(omla_ref.dtype)

        wl = kw_ref.shape[0]
        widx = lax.broadcasted_iota(jnp.int32, (1, qp, wl), 2)
        wpos = past - win_buf + widx
        tw = past + jnp.minimum(lax.broadcasted_iota(jnp.int32, (1, qp, wl), 1), nq - 1)
        win_ok = (widx < win_buf + nq) & (wpos >= tw - WINDOW) & (wpos <= tw)
        shp = (gg, 1, qp, gg * npad)
        ncol = lax.broadcasted_iota(jnp.int32, shp, 3)
        nrow_g = lax.broadcasted_iota(jnp.int32, shp, 0)
        nq_i = lax.broadcasted_iota(jnp.int32, shp, 2)
        nkk = ncol & (npad - 1)
        sel_new = (((ncol >> NEW_SHIFT) == nrow_g) & (nkk <= nq_i) & (nkk < nq)
                   & (sell_ref[:, 0:gg * npad].reshape(shp) > 0.5))
        s2n = _dot_nt(qn, ksn_ref[...]).reshape(gg, jj, qp, gg * npad)
        s2n = jnp.where(sel_new, s2n, NEG_INF).reshape(gg * jj * qp, gg * npad)
        online(m2_ref, l2_ref, acc2_ref, s2n, vsn_ref[...])
        o_s_all = acc2_ref[...] / l2_ref[...]
        for g in range(NSA_KV_GROUPS):
            cols = slice(g * NSA_DH, (g + 1) * NSA_DH)
            rws = slice(g * jj * qp, (g + 1) * jj * qp)
            qg = qn[rws]
            o_s = o_s_all[rws]

            sw = _dot_nt(qg, kw_ref[:, cols]).reshape(jj, qp, wl)
            sw = jnp.where(win_ok, sw, NEG_INF).reshape(jj * qp, wl)
            pw = jnp.exp2(sw - jnp.max(sw, axis=-1, keepdims=True))
            pw = pw / jnp.sum(pw, axis=-1, keepdims=True)
            o_w = _dot(pw.astype(BF16), vw_ref[:, cols])

            gt = gt_ref[rws, :]
            o = gt[:, 0:1] * oc_ref[rws, :] + gt[:, 1:2] * o_s + gt[:, 2:3] * o_w
            for j in range(jj):
                hcol = (g * jj + j) * NSA_DH
                onsa_ref[:, hcol:hcol + NSA_DH] = o[j * qp:(j + 1) * qp].astype(onsa_ref.dtype)


def _sample_attend(page_table, ckv_pool, kr_pool, ks_pool, vs_pool, ql, qr, qn, selc, sell, oc, gt,
                   ckvn, krn, ksn, vsn, kw, vw, wuv, past, nq, win_buf):
    db, npages = page_table.shape
    pps = min(ATTEND_PAGES_PER_STEP, npages)
    nch = npages // pps
    kk = pps * PAGE_ROWS
    qp = QUERY_PAD
    hh = MLA_HEADS
    rows = NSA_HPG * qp

    def pages(rows_, width):
        return [pl.BlockSpec((None, rows_, width),
                             lambda b, c, pt, p=p: (pt[b * npages + c * pps + p], 0, 0)) for p in range(pps)]

    def per_seq(shape):
        nd = len(shape)
        return pl.BlockSpec((None,) + tuple(shape), lambda b, c, pt: (b,) + (0,) * nd)

    def const(shape):
        nd = len(shape)
        return pl.BlockSpec(tuple(shape), lambda b, c, pt: (0,) * nd)

    in_specs = (
        pages(PAGE_ROWS, MLA_KV_RANK) + pages(MLA_ROPE, PAGE_ROWS)
        + pages(PAGE_IROWS, NSA_DH) + pages(PAGE_IROWS, NSA_DH) + [
            per_seq((hh * qp, MLA_KV_RANK)),
            per_seq((hh * qp, MLA_ROPE)),
            per_seq((NSA_KV_GROUPS * rows, NSA_DH)),
            pl.BlockSpec((None, None, NSA_KV_GROUPS * qp, LANE), lambda b, c, pt: (b, c, 0, 0)),
            per_seq((NSA_KV_GROUPS * qp, LANE)),
            per_seq((NSA_KV_GROUPS * rows, NSA_DH)),
            per_seq((NSA_KV_GROUPS * rows, 3)),
            per_seq((NEW_PAD, MLA_KV_RANK)),
            per_seq((NEW_PAD, MLA_ROPE)),
            per_seq((NSA_KV_GROUPS * NEW_PAD, NSA_DH)),
            per_seq((NSA_KV_GROUPS * NEW_PAD, NSA_DH)),
            per_seq(kw.shape[1:]),
            per_seq(vw.shape[1:]),
            const(wuv.shape),
        ])
    grid_spec = pltpu.PrefetchScalarGridSpec(
        num_scalar_prefetch=1,
        grid=(db, nch),
        in_specs=in_specs,
        out_specs=[per_seq((qp, hh * MLA_V)), per_seq((qp, NSA_HEADS * NSA_DH))],
        scratch_shapes=[
            pltpu.VMEM((kk, MLA_KV_RANK), BF16),
            pltpu.VMEM((MLA_ROPE, kk), BF16),
            pltpu.VMEM((NSA_KV_GROUPS * kk, NSA_DH), BF16),
            pltpu.VMEM((NSA_KV_GROUPS * kk, NSA_DH), BF16),
            pltpu.VMEM((hh * qp, 1), F32),
            pltpu.VMEM((hh * qp, 1), F32),
            pltpu.VMEM((hh * qp, MLA_KV_RANK), F32),
            pltpu.VMEM((NSA_KV_GROUPS * rows, 1), F32),
            pltpu.VMEM((NSA_KV_GROUPS * rows, 1), F32),
            pltpu.VMEM((NSA_KV_GROUPS * rows, NSA_DH), F32),
        ],
    )
    return pl.pallas_call(
        functools.partial(_sample_attend_kernel, pps=pps, nch=nch, past=past, nq=nq, win_buf=win_buf),
        grid_spec=grid_spec,
        out_shape=[
            jax.ShapeDtypeStruct((db, qp, hh * MLA_V), BF16),
            jax.ShapeDtypeStruct((db, qp, NSA_HEADS * NSA_DH), BF16),
        ],
        compiler_params=_params("parallel", "arbitrary"),
        name="sample_attend",
    )(page_table.reshape(-1), *([ckv_pool] * pps), *([kr_pool] * pps), *([ks_pool] * pps), *([vs_pool] * pps),
      ql, qr, qn, selc, sell, oc, gt, ckvn, krn, ksn, vsn, kw, vw, wuv)


def _outproj_kernel(x_ref, a_ref, b_ref, wa_ref, wb_ref, o_ref):
    o_ref[...] = x_ref[...] + _dot(a_ref[...], wa_ref[...]) + _dot(b_ref[...], wb_ref[...])


def _outproj(x, a, b, wa, wb):
    n, d = x.shape
    tm = _row_tile(n, 512)
    return pl.pallas_call(
        _outproj_kernel,
        grid=(n // tm,),
        in_specs=[
            pl.BlockSpec((tm, d), lambda i: (i, 0)),
            pl.BlockSpec((tm, a.shape[1]), lambda i: (i, 0)),
            pl.BlockSpec((tm, b.shape[1]), lambda i: (i, 0)),
            pl.BlockSpec(wa.shape, lambda i: (0, 0)),
            pl.BlockSpec(wb.shape, lambda i: (0, 0)),
        ],
        out_specs=pl.BlockSpec((tm, d), lambda i: (i, 0)),
        out_shape=jax.ShapeDtypeStruct((n, d), F32),
        compiler_params=_params("parallel"),
        name="outproj",
    )(x, a, b, wa, wb)


def _rope_tables(pos, rot_dim):
    inv = ROPE_THETA ** (-jnp.arange(0, rot_dim, 2, dtype=F32) / rot_dim)
    ang = pos.astype(F32)[:, None] * inv[None, :]
    return jnp.cos(ang), jnp.sin(ang)


def _rope(x, cos, sin):
    half = cos.shape[-1]
    c = cos[:, None, :]
    s = sin[:, None, :]
    x1, x2 = x[..., :half], x[..., half:2 * half]
    return jnp.concatenate([x1 * c - x2 * s, x2 * c + x1 * s, x[..., 2 * half:]], axis=-1)


def _pad_queries(a, db, nq):
    heads, d = a.shape[1:]
    a = a.reshape(db, nq, heads, d).transpose(0, 2, 1, 3)
    a = jnp.pad(a, ((0, 0), (0, 0), (0, QUERY_PAD - nq), (0, 0)))
    return a.reshape(db, heads * QUERY_PAD, d)


def _pad_new(a, db, nq):
    a = a.reshape(db, nq, a.shape[-1])
    return jnp.pad(a, ((0, 0), (0, NEW_PAD - nq), (0, 0))).astype(BF16)


def _pad_new_groups(a, db, nq):
    a = a.reshape(db, nq, NSA_KV_GROUPS, NSA_DH).transpose(0, 2, 1, 3)
    a = jnp.pad(a, ((0, 0), (0, 0), (0, NEW_PAD - nq), (0, 0)))
    return a.reshape(db, NSA_KV_GROUPS * NEW_PAD, NSA_DH).astype(BF16)


def kernel(x_prompt, x_sample, cache_mla_ckv, cache_mla_krope, cache_nsa_k_cmp, cache_nsa_v_cmp, cache_nsa_k_sel, cache_nsa_v_sel, state_nsa_k_win, state_nsa_v_win, page_table, ffn1_norm, w_ffn1_gate, w_ffn1_up, w_ffn1_down, mix_norm, w_in, mla_q_norm, w_mla_q_up, mla_kv_norm, w_mla_k_up, w_mla_v_up, nsa_cmp_pos_k, nsa_cmp_lin_k, nsa_cmp_pos_v, nsa_cmp_lin_v, w_out, ffn2_norm, w_ffn2_gate, w_ffn2_up, w_ffn2_down, final_norm):
    nb, seq, d = x_prompt.shape
    db, nq = x_sample.shape[:2]
    depth, n_pool = cache_mla_ckv.shape[:2]
    npages = page_table.shape[1]
    past = npages * PAGE_ROWS
    win_buf = state_nsa_k_win.shape[2]
    np_tok = nb * seq
    ns_tok = db * nq
    assert nq <= QUERY_PAD and cache_mla_ckv.shape[2] == PAGE_ROWS and seq % QUERY_BLOCK == 0
    hh, gg, dh = MLA_HEADS, NSA_KV_GROUPS, NSA_DH
    x = jnp.concatenate([x_prompt.reshape(np_tok, d), x_sample.reshape(ns_tok, d)], axis=0)
    pos = jnp.concatenate([jnp.tile(jnp.arange(seq), nb), jnp.tile(past + jnp.arange(nq), db)])
    cos_m, sin_m = _rope_tables(pos, MLA_ROPE)
    cos_n, sin_n = _rope_tables(pos, NSA_ROT)

    ns_s = -(-(past + nq) // SEL_BLOCK)
    selw = -(-ns_s // LANE) * LANE
    nseg_s = past // CMP_STRIDE
    c_st = np.arange(nseg_s)[:, None] * CMP_STRIDE
    s_st = np.arange(selw)[None, :] * SEL_BLOCK
    overlap = jnp.asarray(((c_st < s_st + SEL_BLOCK) & (c_st + CMP_BLOCK > s_st)), dtype=BF16)
    pps = min(ATTEND_PAGES_PER_STEP, npages)
    nch = npages // pps
    bpc = pps * PAGE_ROWS // SEL_BLOCK
    assert bpc <= LANE

    rows_p = [[] for _ in range(8)]
    rows_s = [[] for _ in range(8)]
    for l in range(depth):
        wq = w_mla_q_up[l].reshape(MLA_Q_RANK, hh, MLA_NOPE + MLA_ROPE)
        wqn = wq[:, :, :MLA_NOPE].reshape(MLA_Q_RANK, hh * MLA_NOPE).astype(BF16)
        wqr = wq[:, :, MLA_NOPE:].reshape(MLA_Q_RANK, hh * MLA_ROPE).astype(BF16)
        wuk = w_mla_k_up[l].reshape(MLA_KV_RANK, hh, MLA_NOPE).transpose(1, 2, 0).astype(BF16)
        wuv = w_mla_v_up[l].reshape(MLA_KV_RANK, hh, MLA_V).transpose(1, 0, 2).astype(BF16)
        wi = w_in[l]
        o_q, o_kv, o_kr = 0, MLA_Q_RANK, MLA_Q_RANK + MLA_KV_RANK
        o_qn = o_kr + MLA_ROPE
        o_kv6 = o_qn + NSA_HEADS * dh
        o_g = o_kv6 + 6 * KV_COLS
        w_in_p = jnp.concatenate([
            wi[:, o_q:o_kr], wi[:, o_qn:o_g], wi[:, o_kr:o_qn], wi[:, o_g:],
            jnp.zeros((d, IN_PAD - wi.shape[1]), wi.dtype)], axis=1).astype(BF16)
        pwk, pwv = (jnp.repeat(w, NSA_KV_GROUPS, axis=0) for w in (nsa_cmp_pos_k[l], nsa_cmp_pos_v[l]))
        link = nsa_cmp_lin_k[l].astype(BF16)
        linv = nsa_cmp_lin_v[l].astype(BF16)

        x = _ffn(x, ffn1_norm[l], w_ffn1_gate[l].astype(BF16), w_ffn1_up[l].astype(BF16),
                 w_ffn1_down[l].astype(BF16))
        z = _inproj(x, mix_norm[l], w_in_p)
        ql, qr_raw, ckv = _mlaprep(z, mla_q_norm[l], mla_kv_norm[l], wqn, wqr, wuk)

        qr = _rope(qr_raw.reshape(-1, hh, MLA_ROPE), cos_m, sin_m) * MLA_QSCALE
        krope = _rope(z[:, None, COL_KR:COL_KR + MLA_ROPE], cos_m, sin_m)[:, 0]
        qn = _rope(z[:, COL_QN:COL_KC].reshape(-1, NSA_HEADS, dh), cos_n, sin_n) * NSA_QSCALE

        def kv_rot(col):
            return _rope(z[:, col:col + KV_COLS].reshape(-1, gg, dh), cos_n, sin_n).reshape(-1, KV_COLS)

        k_cmp, k_sel, k_win = kv_rot(COL_KC), kv_rot(COL_KS), kv_rot(COL_KW)
        v_cmp, v_sel, v_win = (z[:, c0:c0 + KV_COLS] for c0 in (COL_VC, COL_VS, COL_VW))
        ckv_b, krope_b = ckv.astype(BF16), krope.astype(BF16)
        k_sel_b, v_sel_b, k_win_b, v_win_b = (a.astype(BF16) for a in (k_sel, v_sel, k_win, v_win))

        qr_t = qr.transpose(1, 0, 2).astype(BF16)
        qn_t = qn.transpose(1, 0, 2).astype(BF16)
        o_mla_p = _mla_prompt(ql, qr_t, ckv_b, krope_b, wuv, nb, seq)
        kcmp_p, vcmp_p = _compress_prompt(k_cmp, v_cmp, pwk, pwv, link, linv, nb, seq)
        o_nsa_p = _nsa_prompt(qn_t, kcmp_p, vcmp_p, k_sel_b, v_sel_b, k_win_b, v_win_b, z, nb, seq)

        sm = slice(np_tok, None)
        ql_s = _pad_queries(ql[:, sm].transpose(1, 0, 2), db, nq)
        qr_s = _pad_queries(qr[sm], db, nq).astype(BF16)
        qn_s = _pad_queries(qn[sm], db, nq).astype(BF16)
        gates = jax.nn.sigmoid(z[sm, COL_G:COL_G + GATE_COLS]).reshape(ns_tok, NSA_HEADS, 3)
        gt_s = _pad_queries(gates, db, nq)
        ckv_pool = cache_mla_ckv[l]
        kr_pool = jnp.swapaxes(cache_mla_krope[l], 1, 2)
        kc_pool, vc_pool, ks_pool, vs_pool = (c[l].reshape(n_pool, PAGE_IROWS, dh) for c in (
            cache_nsa_k_cmp, cache_nsa_v_cmp, cache_nsa_k_sel, cache_nsa_v_sel))
        o_c, sel = _sample_select(page_table, kc_pool, vc_pool, qn_s.reshape(db, gg, NSA_HPG * QUERY_PAD, dh),
                                  pwk, pwv, link, linv, overlap, past, nq, ns_s, min(SEL_TOPK, ns_s))
        selc = sel[:, :, :nch * bpc].reshape(db, gg * QUERY_PAD, nch, bpc).transpose(0, 2, 1, 3)
        selc = jnp.pad(selc, ((0, 0), (0, 0), (0, 0), (0, LANE - bpc)))
        sell = jnp.broadcast_to(sel[:, :, nch * bpc:nch * bpc + 1], (db, gg * QUERY_PAD, LANE))
        kwin_full = jnp.concatenate([state_nsa_k_win[l].reshape(db, win_buf, KV_COLS),
                                     k_win[sm].reshape(db, nq, KV_COLS)], axis=1)
        vwin_full = jnp.concatenate([state_nsa_v_win[l].reshape(db, win_buf, KV_COLS),
                                     v_win[sm].reshape(db, nq, KV_COLS)], axis=1)
        wpad = ((0, 0), (0, NEW_PAD - nq), (0, 0))
        o_mla_s, o_nsa_s = _sample_attend(
            page_table, ckv_pool, kr_pool, ks_pool, vs_pool, ql_s, qr_s, qn_s, selc, sell,
            o_c.reshape(db, -1, dh), gt_s,
            _pad_new(ckv[sm], db, nq), _pad_new(krope[sm], db, nq), _pad_new_groups(k_sel[sm], db, nq),
            _pad_new_groups(v_sel[sm], db, nq), jnp.pad(kwin_full, wpad).astype(BF16),
            jnp.pad(vwin_full, wpad).astype(BF16), wuv, past, nq, win_buf)

        mix_a = jnp.concatenate([o_mla_p, o_mla_s[:, :nq].reshape(ns_tok, -1)], axis=0)
        mix_b = jnp.concatenate([o_nsa_p, o_nsa_s[:, :nq].reshape(ns_tok, -1)], axis=0)
        wo = w_out[l].astype(BF16)
        x = _outproj(x, mix_a, mix_b, wo[:hh * MLA_V], wo[hh * MLA_V:])
        x = _ffn(x, ffn2_norm[l], w_ffn2_gate[l].astype(BF16), w_ffn2_up[l].astype(BF16),
                 w_ffn2_down[l].astype(BF16), final_g=final_norm if l == depth - 1 else None)

        win_p = min(WINDOW, seq)
        new = (ckv, krope, k_cmp, v_cmp, k_sel, v_sel)
        for i, a in enumerate(new):
            tail = a.shape[1:] if i < 2 else (gg, dh)
            rows_p[i].append(a[:np_tok].reshape((nb, seq) + tail))
            rows_s[i].append(a[np_tok:].reshape((db, nq) + tail))
        rows_p[6].append(k_win[:np_tok].reshape(nb, seq, gg, dh)[:, -win_p:])
        rows_p[7].append(v_win[:np_tok].reshape(nb, seq, gg, dh)[:, -win_p:])
        rows_s[6].append(kwin_full[:, -win_buf:].reshape(db, win_buf, gg, dh))
        rows_s[7].append(vwin_full[:, -win_buf:].reshape(db, win_buf, gg, dh))

    y_prompt = x[:np_tok].reshape(nb, seq, d)
    y_sample = x[np_tok:].reshape(db, nq, d)
    out = [y_prompt, y_sample]
    for i in range(8):
        out += [jnp.stack(rows_p[i]), jnp.stack(rows_s[i])]
    return tuple(out)
```
